```python
import math
import jax, jax.numpy as jnp
from jax import lax
import numpy as np

D_MODEL = 2048
BATCH = 32
SEQ = 256
DEPTH = 2
DEC_BATCH = 2
DEC_SEQ = 4096
PAST_LEN = 512

GRID_W = 64
WA = 2048
CONV_A_W = 3
SSD_HEADS = 64
SSD_HEAD_DIM = 64
WB = SSD_HEADS * SSD_HEAD_DIM
SSD_GROUPS = 8
SSD_HPG = SSD_HEADS // SSD_GROUPS
SSD_STATE = 128
SSD_CONV_W = 5
SSD_CHUNK = 128
XBC_W = WB + 2 * SSD_GROUPS * SSD_STATE
WC = 2048
CONF_CONV_W = 31
SPLIT_SIZES = (WA, WA, WA, WA, WB, XBC_W, 2 * SSD_HEADS, 2 * WC, WC, 3 * D_MODEL)
N_IN = sum(SPLIT_SIZES)
SPLIT_POINTS = tuple(int(v) for v in np.cumsum(SPLIT_SIZES)[:-1])
EPS = 1e-6

kernel_name = 'hybrid_gatedconv_ssd_conformer_flow_step'


def rmsnorm(x, g):
    xf = x.astype(jnp.float32)
    y = xf * lax.rsqrt(jnp.mean(xf * xf, axis=-1, keepdims=True) + EPS)
    return y.astype(x.dtype) * g


def layernorm(x, g, b):
    xf = x.astype(jnp.float32)
    mu = jnp.mean(xf, axis=-1, keepdims=True)
    var = jnp.mean(jnp.square(xf - mu), axis=-1, keepdims=True)
    return ((xf - mu) * lax.rsqrt(var + EPS)).astype(x.dtype) * g + b


def dwconv(x, w, b, rows):
    n, L, C = x.shape
    if rows is not None:
        x = x.reshape(n * rows, L // rows, C)
    k = w.shape[0]
    y = lax.conv_general_dilated(x, w[:, None, :], window_strides=(1,),
                                 padding=[(k // 2, k // 2)],
                                 dimension_numbers=('NWC', 'WIO', 'NWC'),
                                 feature_group_count=C)
    if b is not None:
        y = y + b
    return y.reshape(n, L, C)


def ssd_chunked(x, dt, A, bm, cm, h0):
    b, L = x.shape[:2]
    nc = L // SSD_CHUNK
    x = x.reshape(b, nc, SSD_CHUNK, SSD_GROUPS, SSD_HPG, SSD_HEAD_DIM)
    dt = dt.reshape(b, nc, SSD_CHUNK, SSD_GROUPS, SSD_HPG)
    bm = bm.reshape(b, nc, SSD_CHUNK, SSD_GROUPS, SSD_STATE)
    cm = cm.reshape(b, nc, SSD_CHUNK, SSD_GROUPS, SSD_STATE)
    a_cum = jnp.cumsum(jnp.moveaxis(dt * A, 2, -1), axis=-1)
    xdt = x * dt[..., None]
    tril = jnp.tril(jnp.ones((SSD_CHUNK, SSD_CHUNK), dtype=bool))
    decay = jnp.exp(jnp.where(tril, a_cum[..., :, None] - a_cum[..., None, :], -jnp.inf))
    scores = jnp.einsum('bcign,bcjgn->bcgij', cm, bm)
    y_diag = jnp.einsum('bcgij,bcgkij,bcjgkp->bcigkp', scores, decay, xdt)
    decay_end = jnp.exp(a_cum[..., -1:] - a_cum)
    states = jnp.einsum('bcjgn,bcgkj,bcjgkp->bcgkpn', bm, decay_end, xdt)
    chunk_decay = jnp.exp(a_cum[..., -1])

    def step(h, inp):
        s, d = inp
        return h * d[..., None, None] + s, h

    h_last, h_prev = lax.scan(step, h0.astype(jnp.float32),
                              (jnp.moveaxis(states, 1, 0), jnp.moveaxis(chunk_decay, 1, 0)))
    h_prev = jnp.moveaxis(h_prev, 0, 1)
    y_off = jnp.einsum('bcign,bcgkpn,bcgki->bcigkp', cm, h_prev, jnp.exp(a_cum))
    y = (y_diag + y_off).reshape(b, L, SSD_GROUPS, SSD_HPG, SSD_HEAD_DIM)
    return y, h_last


def ssd_bidir(xs, dt_raw, bm, cm, dt_bias, a_log, d_skip, h0_f, h0_b):
    b, L = xs.shape[:2]
    x = xs.reshape(b, L, SSD_GROUPS, SSD_HPG, SSD_HEAD_DIM)
    bm = bm.reshape(b, L, SSD_GROUPS, SSD_STATE)
    cm = cm.reshape(b, L, SSD_GROUPS, SSD_STATE)
    dt = jax.nn.softplus(dt_raw.astype(jnp.float32).reshape(b, L, 2, SSD_GROUPS, SSD_HPG)
                         + dt_bias.astype(jnp.float32).reshape(2, SSD_GROUPS, SSD_HPG))
    A = -jnp.exp(a_log.astype(jnp.float32)).reshape(2, SSD_GROUPS, SSD_HPG)
    y_f, h_f = ssd_chunked(x, dt[:, :, 0], A[0], bm, cm, h0_f)
    flip = lambda t: jnp.flip(t, axis=1)
    y_b, h_b = ssd_chunked(flip(x), flip(dt[:, :, 1]), A[1], flip(bm), flip(cm), h0_b)
    y = y_f + flip(y_b) + x * d_skip.reshape(SSD_GROUPS, SSD_HPG, 1)
    return y.reshape(b, L, WB).astype(xs.dtype), h_f, h_b


def mixer_layer(x, mod, rows, h0_f, h0_b, g_pre, g_post, w_in, conv_a_w, ssd_conv_w, ssd_conv_b,
                dt_bias, a_log, d_skip, ssd_norm_g, conf_conv_w, conf_conv_b, conf_ln_g, conf_ln_b,
                w_pa, w_pb, w_pc, w_o):
    shift, scale, gate = jnp.split(mod, 3, axis=-1)
    u = rmsnorm(x, g_pre) * (1 + scale) + shift
    proj = jnp.einsum('bld,dn->bln', u, w_in)
    a_b, a_c, a_h, a_z, b_z, b_xbc, b_dt, c_glu, c_z, g_br = jnp.split(proj, SPLIT_POINTS, axis=-1)
    ya = a_b * dwconv(a_c * a_h, conv_a_w, None, rows)
    pa = (ya * jax.nn.silu(a_z)) @ w_pa
    xbc = jax.nn.silu(dwconv(b_xbc, ssd_conv_w, ssd_conv_b, rows))
    xs, bm, cm = jnp.split(xbc, [WB, WB + SSD_GROUPS * SSD_STATE], axis=-1)
    yb, h_f, h_b = ssd_bidir(xs, b_dt, bm, cm, dt_bias, a_log, d_skip, h0_f, h0_b)
    pb = rmsnorm(yb * jax.nn.silu(b_z), ssd_norm_g) @ w_pb
    c_a, c_g = jnp.split(c_glu, 2, axis=-1)
    yc = dwconv(c_a * jax.nn.sigmoid(c_g), conf_conv_w, conf_conv_b, rows)
    yc = jax.nn.silu(layernorm(yc, conf_ln_g, conf_ln_b))
    pc = (yc * jax.nn.silu(c_z)) @ w_pc
    ga, gb, gc = jnp.split(jax.nn.sigmoid(g_br), 3, axis=-1)
    m = (ga * pa + gb * pb + gc * pc) @ w_o
    return x + gate * rmsnorm(m, g_post), h_f, h_b


def setup_inputs(seed: int = 0) -> dict:
    key = jax.random.key(seed)
    ks = jax.random.split(key, 32)
    nrm = lambda k, shape, s: jax.random.normal(k, shape, jnp.float32) * s
    dt0 = jnp.exp(jax.random.uniform(ks[10], (DEPTH, 2, SSD_HEADS), jnp.float32,
                                     minval=math.log(1e-3), maxval=math.log(1e-1)))
    dt_bias = dt0 + jnp.log(-jnp.expm1(-dt0))
    a_log = jnp.log(jax.random.uniform(ks[11], (DEPTH, 2, SSD_HEADS), jnp.float32, minval=1.0, maxval=16.0))
    return {
        'x_prompt': nrm(ks[0], (BATCH, SEQ, D_MODEL), 1.0),
        'x_sample': nrm(ks[1], (DEC_BATCH, DEC_SEQ, D_MODEL), 1.0),
        'state_ssd': nrm(ks[2], (DEC_BATCH, DEPTH, 2, SSD_HEADS, SSD_HEAD_DIM, SSD_STATE), 0.1),
        'c': nrm(ks[3], (DEC_BATCH, D_MODEL), 1.0),
        'c_ctx': nrm(ks[4], (D_MODEL,), 1.0),
        'w_mod': nrm(ks[5], (DEPTH, D_MODEL, 3 * D_MODEL), 0.5 * D_MODEL ** -0.5),
        'b_mod': nrm(ks[6], (DEPTH, 3 * D_MODEL), 0.02),
        'g_pre': 1.0 + nrm(ks[7], (DEPTH, D_MODEL), 0.02),
        'g_post': 1.0 + nrm(ks[8], (DEPTH, D_MODEL), 0.02),
        'w_in': nrm(ks[9], (DEPTH, D_MODEL, N_IN), D_MODEL ** -0.5),
        'conv_a_w': nrm(ks[12], (DEPTH, CONV_A_W, WA), CONV_A_W ** -0.5),
        'ssd_conv_w': nrm(ks[13], (DEPTH, SSD_CONV_W, XBC_W), SSD_CONV_W ** -0.5),
        'ssd_conv_b': nrm(ks[14], (DEPTH, XBC_W), 0.02),
        'dt_bias': dt_bias,
        'a_log': a_log,
        'd_skip': 1.0 + nrm(ks[15], (DEPTH, SSD_HEADS), 0.1),
        'ssd_norm_g': 1.0 + nrm(ks[16], (DEPTH, WB), 0.02),
        'conf_conv_w': nrm(ks[17], (DEPTH, CONF_CONV_W, WC), CONF_CONV_W ** -0.5),
        'conf_conv_b': nrm(ks[18], (DEPTH, WC), 0.02),
        'conf_ln_g': 1.0 + nrm(ks[19], (DEPTH, WC), 0.02),
        'conf_ln_b': nrm(ks[20], (DEPTH, WC), 0.02),
        'w_pa': nrm(ks[21], (DEPTH, WA, D_MODEL), WA ** -0.5),
        'w_pb': nrm(ks[22], (DEPTH, WB, D_MODEL), WB ** -0.5),
        'w_pc': nrm(ks[23], (DEPTH, WC, D_MODEL), WC ** -0.5),
        'w_o': nrm(ks[24], (DEPTH, D_MODEL, D_MODEL), D_MODEL ** -0.5),
    }


def reference(x_prompt, x_sample, state_ssd, c, c_ctx, w_mod, b_mod, g_pre, g_post, w_in, conv_a_w,
              ssd_conv_w, ssd_conv_b, dt_bias, a_log, d_skip, ssd_norm_g, conf_conv_w, conf_conv_b,
              conf_ln_g, conf_ln_b, w_pa, w_pb, w_pc, w_o):
    nb_p = x_prompt.shape[0]
    nb_s = x_sample.shape[0]
    rows = x_sample.shape[1] // GRID_W
    hshape = (SSD_GROUPS, SSD_HPG, SSD_HEAD_DIM, SSD_STATE)
    h_zero = jnp.zeros((nb_p,) + hshape, jnp.float32)
    yp, ys = x_prompt, x_sample
    new_states = []
    for l in range(DEPTH):
        lw = (g_pre[l], g_post[l], w_in[l], conv_a_w[l], ssd_conv_w[l], ssd_conv_b[l], dt_bias[l],
              a_log[l], d_skip[l], ssd_norm_g[l], conf_conv_w[l], conf_conv_b[l], conf_ln_g[l],
              conf_ln_b[l], w_pa[l], w_pb[l], w_pc[l], w_o[l])
        mod_ctx = (jax.nn.silu(c_ctx) @ w_mod[l] + b_mod[l])[None, None, :]
        yp, h_f, h_b = mixer_layer(yp, mod_ctx, None, h_zero, h_zero, *lw)
        new_states.append(jnp.stack([h_f, h_b], axis=1).reshape(nb_p, 2, SSD_HEADS, SSD_HEAD_DIM, SSD_STATE))
        mod_lat = (jax.nn.silu(c) @ w_mod[l] + b_mod[l])[:, None, :]
        h0_f = state_ssd[:, l, 0].reshape((nb_s,) + hshape)
        h0_b = state_ssd[:, l, 1].reshape((nb_s,) + hshape)
        ys, _, _ = mixer_layer(ys, mod_lat, rows, h0_f, h0_b, *lw)
    new_state_ssd = jnp.stack(new_states, axis=1).astype(x_prompt.dtype)
    return (yp, ys, new_state_ssd)
```

```python
import functools

import jax
import jax.numpy as jnp
from jax import lax
from jax.experimental import pallas as pl
from jax.experimental.pallas import tpu as pltpu

GRID_W = 64
SSD_CHUNK = 128
EPS = 1e-6
LANES = 128
V7X_VMEM_LIMIT_BYTES = 56 * 1024 * 1024

F32 = jnp.float32
BF16 = jnp.bfloat16


def _params(*sem):
    return pltpu.CompilerParams(dimension_semantics=sem, vmem_limit_bytes=V7X_VMEM_LIMIT_BYTES)


def _silu(x):
    return x * jax.nn.sigmoid(x)


def _dot(a, b):
    return jnp.dot(a, b, preferred_element_type=F32)


def _split3(x):
    hi = x.astype(BF16)
    r1 = x - hi.astype(F32)
    mid = r1.astype(BF16)
    lo = (r1 - mid.astype(F32)).astype(BF16)
    return hi, mid, lo


def _dot_exact_lhs(x, sel):
    hi, mid, lo = _split3(x)
    return _dot(hi, sel) + _dot(mid, sel) + _dot(lo, sel)


def _dot_exact_rhs(sel, x):
    hi, mid, lo = _split3(x)
    return _dot(sel, hi) + _dot(sel, mid) + _dot(sel, lo)


def _mod_kernel(c_ref, w_ref, b_ref, o_ref):
    s = _silu(c_ref[...]).astype(BF16)
    o_ref[...] = _dot(s, w_ref[...].astype(BF16)) + b_ref[...]


def _mod_call(cvec, w_mod, b_mod):
    depth, d, n3 = w_mod.shape
    rows = cvec.shape[0]
    tn = min(1024, n3)
    return pl.pallas_call(
        _mod_kernel,
        grid=(depth, n3 // tn),
        in_specs=[pl.BlockSpec((rows, d), lambda l, j: (0, 0)),
                  pl.BlockSpec((None, d, tn), lambda l, j: (l, 0, j)),
                  pl.BlockSpec((None, 1, tn), lambda l, j: (l, 0, j))],
        out_specs=pl.BlockSpec((None, rows, tn), lambda l, j: (l, 0, j)),
        out_shape=jax.ShapeDtypeStruct((depth, rows, n3), F32),
        compiler_params=_params("arbitrary", "arbitrary"),
        name="mod",
    )(cvec, w_mod, b_mod.reshape(depth, 1, n3))


def _premod_kernel(x_ref, g_ref, mod_ref, u_ref):
    x = x_ref[...]
    d = x.shape[-1]
    y = x * lax.rsqrt(jnp.mean(x * x, axis=-1, keepdims=True) + EPS) * g_ref[...]
    u_ref[...] = (y * (1.0 + mod_ref[:, d:2 * d]) + mod_ref[:, 0:d]).astype(BF16)


def _premod_call(x, g_pre, mod3, mod_row, tm):
    m, d = x.shape
    return pl.pallas_call(
        _premod_kernel,
        grid=(m // tm,),
        in_specs=[pl.BlockSpec((tm, d), lambda i: (i, 0)),
                  pl.BlockSpec((1, d), lambda i: (0, 0)),
                  pl.BlockSpec((None, 1, 3 * d), lambda i: (mod_row(i, tm), 0, 0))],
        out_specs=pl.BlockSpec((tm, d), lambda i: (i, 0)),
        out_shape=jax.ShapeDtypeStruct((m, d), BF16),
        compiler_params=_params("arbitrary"),
        name="premod",
    )(x, g_pre.reshape(1, d), mod3)


def _dwconv_blocked(src_ref, w_ref, seg, rb, emit):
    tm, tc = src_ref.shape
    taps = w_ref.shape[0]
    half = taps // 2
    pos = lax.broadcasted_iota(jnp.int32, (rb, LANES), 0) & (seg - 1)

    def body(b, carry):
        r0 = pl.multiple_of(b * rb, rb)
        for c0 in range(0, tc, LANES):
            xs = src_ref[pl.ds(r0, rb), c0:c0 + LANES]
            acc = xs * w_ref[half:half + 1, c0:c0 + LANES]
            for s in range(-half, half + 1):
                if s == 0:
                    continue
                shifted = pltpu.roll(xs, (-s) % rb, 0)
                valid = (pos < seg - s) if s > 0 else (pos >= -s)
                acc = acc + jnp.where(valid, shifted, 0.0) * w_ref[half + s:half + s + 1, c0:c0 + LANES]
            emit(r0, c0, acc)
        return carry

    lax.fori_loop(0, tm // rb, body, 0)


def _mix_a_kernel(n_prompt_tiles, seg_p, seg_s, rb, u_ref, wb_ref, wc_ref, wh_ref, wz_ref, cw_ref,
                  o_ref, ch_scr, bz_scr):
    i = pl.program_id(0)
    seg = jnp.where(i < n_prompt_tiles, seg_p, seg_s)
    u = u_ref[...]
    ch_scr[...] = _dot(u, wc_ref[...]) * _dot(u, wh_ref[...])
    bz_scr[...] = _dot(u, wb_ref[...]) * _silu(_dot(u, wz_ref[...]))

    def emit(r0, c0, acc):
        o_ref[pl.ds(r0, rb), c0:c0 + LANES] = (acc * bz_scr[pl.ds(r0, rb), c0:c0 + LANES]).astype(BF16)

    _dwconv_blocked(ch_scr, cw_ref, seg, rb, emit)


def _mix_a_call(u, w_main, conv_w, offs, wa, tm, tc, seg_info):
    m, d = u.shape
    ob, oc, oh, oz = (o // tc for o in offs)
    wspec = lambda o: pl.BlockSpec((d, tc), lambda i, j, o=o: (0, o + j))
    n_prompt_tiles, seg_p, seg_s, rb = seg_info
    return pl.pallas_call(
        functools.partial(_mix_a_kernel, n_prompt_tiles, seg_p, seg_s, rb),
        grid=(m // tm, wa // tc),
        in_specs=[pl.BlockSpec((tm, d), lambda i, j: (i, 0)),
                  wspec(ob), wspec(oc), wspec(oh), wspec(oz),
                  pl.BlockSpec((conv_w.shape[0], tc), lambda i, j: (0, j))],
        out_specs=pl.BlockSpec((tm, tc), lambda i, j: (i, j)),
        out_shape=jax.ShapeDtypeStruct((m, wa), BF16),
        scratch_shapes=[pltpu.VMEM((tm, tc), F32), pltpu.VMEM((tm, tc), F32)],
        compiler_params=_params("arbitrary", "arbitrary"),
        name="mix_a",
    )(u, w_main, w_main, w_main, w_main, conv_w)


def _mix_c_kernel(n_prompt_tiles, seg_p, seg_s, rb, ln_rows, u_ref, wa_ref, wg_ref, wz_ref, cw_ref,
                  cb_ref, lg_ref, lb_ref, o_ref, glu_scr, conv_scr, zs_scr):
    i = pl.program_id(0)
    j = pl.program_id(1)
    nj = pl.num_programs(1)
    seg = jnp.where(i < n_prompt_tiles, seg_p, seg_s)
    tm, tc = glu_scr.shape
    n_tc = conv_scr.shape[0]
    u = u_ref[...]
    glu_scr[...] = _dot(u, wa_ref[...]) * jax.nn.sigmoid(_dot(u, wg_ref[...]))
    zs_scr[j] = _silu(_dot(u, wz_ref[...])).astype(BF16)

    def emit(r0, c0, acc):
        conv_scr[j, pl.ds(r0, rb), c0:c0 + LANES] = acc + cb_ref[:, c0:c0 + LANES]

    _dwconv_blocked(glu_scr, cw_ref, seg, rb, emit)

    @pl.when(j == nj - 1)
    def _():
        def body(b, carry):
            r0 = pl.multiple_of(b * ln_rows, ln_rows)
            v = jnp.concatenate([conv_scr[t, pl.ds(r0, ln_rows), :] for t in range(n_tc)], axis=1)
            zs = jnp.concatenate([zs_scr[t, pl.ds(r0, ln_rows), :] for t in range(n_tc)], axis=1)
            mu = jnp.mean(v, axis=-1, keepdims=True)
            cen = v - mu
            var = jnp.mean(cen * cen, axis=-1, keepdims=True)
            y = cen * lax.rsqrt(var + EPS) * lg_ref[...] + lb_ref[...]
            o_ref[pl.ds(r0, ln_rows), :] = (_silu(y) * zs.astype(F32)).astype(BF16)
            return carry

        lax.fori_loop(0, tm // ln_rows, body, 0)


def _mix_c_call(u, w_main, conv_w, conv_b, ln_g, ln_b, offs, wc, tm, tc, seg_info):
    m, d = u.shape
    oa, og, oz = (o // tc for o in offs)
    wspec = lambda o: pl.BlockSpec((d, tc), lambda i, j, o=o: (0, o + j))
    n_prompt_tiles, seg_p, seg_s, rb = seg_info
    full = lambda shape: pl.BlockSpec(shape, lambda i, j: (0, 0))
    return pl.pallas_call(
        functools.partial(_mix_c_kernel, n_prompt_tiles, seg_p, seg_s, rb, 16),
        grid=(m // tm, wc // tc),
        in_specs=[pl.BlockSpec((tm, d), lambda i, j: (i, 0)),
                  wspec(oa), wspec(og), wspec(oz),
                  pl.BlockSpec((conv_w.shape[0], tc), lambda i, j: (0, j)),
                  pl.BlockSpec((1, tc), lambda i, j: (0, j)),
                  full((1, wc)), full((1, wc))],
        out_specs=pl.BlockSpec((tm, wc), lambda i, j: (i, 0)),
        out_shape=jax.ShapeDtypeStruct((m, wc), BF16),
        scratch_shapes=[pltpu.VMEM((tm, tc), F32), pltpu.VMEM((wc // tc, tm, tc), F32),
                        pltpu.VMEM((wc // tc, tm, tc), BF16)],
        compiler_params=_params("arbitrary", "arbitrary"),
        name="mix_c",
    )(u, w_main, w_main, w_main, conv_w, conv_b.reshape(1, wc), ln_g.reshape(1, wc), ln_b.reshape(1, wc))


def _xbc_kernel(n_prompt_tiles, seg_p, seg_s, rb, u_ref, w_ref, cw_ref, cb_ref, o_ref, p_scr):
    i = pl.program_id(0)
    seg = jnp.where(i < n_prompt_tiles, seg_p, seg_s)
    p_scr[...] = _dot(u_ref[...], w_ref[...])

    def emit(r0, c0, acc):
        o_ref[pl.ds(r0, rb), c0:c0 + LANES] = _silu(acc + cb_ref[:, c0:c0 + LANES]).astype(BF16)

    _dwconv_blocked(p_scr, cw_ref, seg, rb, emit)


def _xbc_call(u, w_main, conv_w, conv_b, off, width, tm, tc, seg_info):
    m, d = u.shape
    o = off // tc
    n_prompt_tiles, seg_p, seg_s, rb = seg_info
    return pl.pallas_call(
        functools.partial(_xbc_kernel, n_prompt_tiles, seg_p, seg_s, rb),
        grid=(m // tm, width // tc),
        in_specs=[pl.BlockSpec((tm, d), lambda i, j: (i, 0)),
                  pl.BlockSpec((d, tc), lambda i, j: (0, o + j)),
                  pl.BlockSpec((conv_w.shape[0], tc), lambda i, j: (0, j)),
                  pl.BlockSpec((1, tc), lambda i, j: (0, j))],
        out_specs=pl.BlockSpec((tm, tc), lambda i, j: (i, j)),
        out_shape=jax.ShapeDtypeStruct((m, width), BF16),
        scratch_shapes=[pltpu.VMEM((tm, tc), F32)],
        compiler_params=_params("arbitrary", "arbitrary"),
        name="xbc",
    )(u, w_main, conv_w, conv_b.reshape(1, width))


def _zdt_kernel(u_ref, wz_ref, wdt_ref, dtb_ref, zs_ref, dt_ref):
    u = u_ref[...]
    zs_ref[...] = _silu(_dot(u, wz_ref[...])).astype(BF16)

    @pl.when(pl.program_id(1) == 0)
    def _():
        v = _dot(u, wdt_ref[...]) + dtb_ref[...]
        dt_ref[...] = jnp.maximum(v, 0.0) + jnp.log1p(jnp.exp(-jnp.abs(v)))


def _zdt_call(u, w_main, w_dt, dt_bias, off, width, tm, tc):
    m, d = u.shape
    o = off // tc
    nh2 = w_dt.shape[-1]
    return pl.pallas_call(
        _zdt_kernel,
        grid=(m // tm, width // tc),
        in_specs=[pl.BlockSpec((tm, d), lambda i, j: (i, 0)),
                  pl.BlockSpec((d, tc), lambda i, j: (0, o + j)),
                  pl.BlockSpec((d, nh2), lambda i, j: (0, 0)),
                  pl.BlockSpec((1, nh2), lambda i, j: (0, 0))],
        out_specs=[pl.BlockSpec((tm, tc), lambda i, j: (i, j)),
                   pl.BlockSpec((tm, nh2), lambda i, j: (i, 0))],
        out_shape=[jax.ShapeDtypeStruct((m, width), BF16), jax.ShapeDtypeStruct((m, nh2), F32)],
        compiler_params=_params("arbitrary", "arbitrary"),
        name="zdt",
    )(u, w_main, w_dt, dt_bias.reshape(1, nh2))


def _ssd_kernel(dims, has_h0, emit_state, *refs):
    n_groups, hpg, hd, ns = dims
    refs = list(refs)
    x_ref, bm_ref, cm_ref, dt_ref, alog_ref, dsk_ref, exp_ref = refs[:7]
    refs = refs[7:]
    h0_ref = refs.pop(0) if has_h0 else None
    y_ref = refs.pop(0)
    so_ref = refs.pop(0) if emit_state else None
    state = refs.pop(0)

    d = pl.program_id(1)
    b = pl.program_id(2)
    nb = pl.num_programs(2)
    is_f = d == 0
    bt = x_ref.shape[0]
    q = SSD_CHUNK
    nck = bt // q
    gw = hpg * hd
    pairs = gw // LANES

    @pl.when(b == 0)
    def _():
        if has_h0:
            for g in range(n_groups):
                hg = h0_ref[g * hpg:(g + 1) * hpg].reshape(gw, ns)
                state[g] = hg.T
        else:
            state[...] = jnp.zeros_like(state)

    ii = lax.broadcasted_iota(jnp.int32, (q, q), 0)
    jj = lax.broadcasted_iota(jnp.int32, (q, q), 1)
    sgn = jnp.where(is_f, 1, -1)
    tri = (ii - jj) * sgn >= 0
    cum_mat = jnp.where(tri, 1.0, 0.0).astype(BF16)
    lane = lax.broadcasted_iota(jnp.int32, (q, LANES), 1)
    keep_lo = jnp.where(lane < hd, 1.0, 0.0).astype(BF16)
    keep_hi = jnp.where(lane < hd, 0.0, 1.0).astype(BF16)
    a_sel = -jnp.exp(alog_ref[pl.ds(d, 1), :])
    dsk = dsk_ref[...] * jnp.where(is_f, 1.0, 0.0)

    def chunk(k, carry):
        c = jnp.where(is_f, k, nck - 1 - k)
        r0 = pl.multiple_of(c * q, q)
        rows = pl.ds(r0, q)
        dt_c = dt_ref[rows, :]
        dt_sel = jnp.where(is_f, dt_c, pltpu.roll(dt_c, dt_c.shape[-1] // 2, 1))
        acum = _dot_exact_rhs(cum_mat, dt_sel * a_sel)
        total = jnp.where(is_f, acum[q - 1:q, :], acum[0:1, :])
        w_end = jnp.exp(total - acum) * dt_sel
        acum_t = acum.T
        dt_t = dt_sel.T
        w_end_t = w_end.T
        dec_all = jnp.exp(_dot_exact_lhs(jnp.broadcast_to(total, (8, total.shape[-1])), exp_ref[...])[0:1, :])

        for g in range(n_groups):
            cm_g = cm_ref[rows, g * ns:(g + 1) * ns]
            bm_g = bm_ref[rows, g * ns:(g + 1) * ns]
            scores = lax.dot_general(cm_g, bm_g, (((1,), (1,)), ((), ())), preferred_element_type=F32)
            cm_f = cm_g.astype(F32)
            bm_t = bm_g.astype(F32).T
            for p in range(pairs):
                lanes = slice(g * gw + p * LANES, g * gw + (p + 1) * LANES)
                x_pair = x_ref[rows, lanes]
                x_lo = x_pair * keep_lo
                x_hi = x_pair * keep_hi
                h_pair = state[g, :, p * LANES:(p + 1) * LANES]
                h_bf = h_pair.astype(BF16)
                h_lo = h_bf * keep_lo
                h_hi = h_bf * keep_hi
                m_parts, e_parts, w_parts = [], [], []
                for hh in range(2):
                    h = g * hpg + p * (LANES // hd) + hh
                    colb = jnp.broadcast_to(acum[:, h:h + 1], (q, q))
                    decay = jnp.exp(jnp.where(tri, colb - acum_t[h:h + 1, :], -jnp.inf))
                    m_parts.append((scores * decay * dt_t[h:h + 1, :]).astype(BF16))
                    e_parts.append((jnp.exp(colb) * cm_f).astype(BF16))
                    w_parts.append((bm_t * w_end_t[h:h + 1, :]).astype(BF16))
                lhs_y = jnp.concatenate(m_parts + e_parts, axis=1)
                rhs_y = jnp.concatenate([x_lo, x_hi, h_lo, h_hi], axis=0)
                y = _dot(lhs_y, rhs_y)
                s_new = _dot(jnp.concatenate(w_parts, axis=1), jnp.concatenate([x_lo, x_hi], axis=0))
                state[g, :, p * LANES:(p + 1) * LANES] = h_pair * dec_all[:, lanes] + s_new
                y_ref[rows, lanes] = y + x_pair.astype(F32) * dsk[:, lanes]
        return carry

    lax.fori_loop(0, nck, chunk, 0)

    if emit_state:
        @pl.when(b == nb - 1)
        def _():
            for g in range(n_groups):
                so_ref[g * hpg:(g + 1) * hpg] = state[g].T.reshape(hpg, hd, ns)


def _ssd_call(xbc, dt, a_log2, dsk, expand, h0, dims, row0, nseq, nblk, bt, emit_state):
    n_groups, hpg, hd, ns = dims
    heads = n_groups * hpg
    wb = heads * hd
    gn = n_groups * ns
    m = xbc.shape[0]
    blk0 = row0 // bt

    def rowblk(s, d, b):
        return blk0 + s * nblk + b + d * (nblk - 1 - 2 * b)

    in_specs = [pl.BlockSpec((bt, wb), lambda s, d, b: (rowblk(s, d, b), 0)),
                pl.BlockSpec((bt, gn), lambda s, d, b: (rowblk(s, d, b), wb // gn)),
                pl.BlockSpec((bt, gn), lambda s, d, b: (rowblk(s, d, b), wb // gn + 1)),
                pl.BlockSpec((bt, 2 * heads), lambda s, d, b: (rowblk(s, d, b), 0)),
                pl.BlockSpec((2, 2 * heads), lambda s, d, b: (0, 0)),
                pl.BlockSpec((1, wb), lambda s, d, b: (0, 0)),
                pl.BlockSpec(expand.shape, lambda s, d, b: (0, 0))]
    args = [xbc, xbc, xbc, dt, a_log2, dsk, expand]
    if h0 is not None:
        in_specs.append(pl.BlockSpec((None, None, heads, hd, ns), lambda s, d, b: (s, d, 0, 0, 0)))
        args.append(h0)
    nrows = nseq * nblk * bt
    out_specs = [pl.BlockSpec((None, bt, wb), lambda s, d, b: (d, rowblk(s, d, b) - blk0, 0))]
    out_shape = [jax.ShapeDtypeStruct((2, nrows, wb), F32)]
    if emit_state:
        out_specs.append(pl.BlockSpec((None, None, heads, hd, ns), lambda s, d, b: (s, d, 0, 0, 0)))
        out_shape.append(jax.ShapeDtypeStruct((nseq, 2, heads, hd, ns), F32))
    return pl.pallas_call(
        functools.partial(_ssd_kernel, dims, h0 is not None, emit_state),
        grid=(nseq, 2, nblk),
        in_specs=in_specs,
        out_specs=out_specs,
        out_shape=out_shape,
        scratch_shapes=[pltpu.VMEM((n_groups, ns, hpg * hd), F32)],
        compiler_params=_params("arbitrary", "arbitrary", "arbitrary"),
        name="ssd",
    )(*args)


def _ssdnorm_kernel(rows, y_ref, zs_ref, g_ref, o_ref):
    tm = o_ref.shape[0]

    def body(b, carry):
        r0 = pl.multiple_of(b * rows, rows)
        v = (y_ref[0, pl.ds(r0, rows), :] + y_ref[1, pl.ds(r0, rows), :]) * zs_ref[pl.ds(r0, rows), :].astype(F32)
        o_ref[pl.ds(r0, rows), :] = (v * lax.rsqrt(jnp.mean(v * v, axis=-1, keepdims=True) + EPS)
                                     * g_ref[...]).astype(BF16)
        return carry

    lax.fori_loop(0, tm // rows, body, 0)


def _ssdnorm_call(y2, zs, g, row0, tm):
    _, mrows, wb = y2.shape
    blk0 = row0 // tm
    return pl.pallas_call(
        functools.partial(_ssdnorm_kernel, 16),
        grid=(mrows // tm,),
        in_specs=[pl.BlockSpec((2, tm, wb), lambda i: (0, i, 0)),
                  pl.BlockSpec((tm, wb), lambda i: (blk0 + i, 0)),
                  pl.BlockSpec((1, wb), lambda i: (0, 0))],
        out_specs=pl.BlockSpec((tm, wb), lambda i: (i, 0)),
        out_shape=jax.ShapeDtypeStruct((mrows, wb), BF16),
        compiler_params=_params("arbitrary"),
        name="ssdnorm",
    )(y2, zs, g.reshape(1, wb))


def _merge_kernel(u_ref, ya_ref, yb_ref, yc_ref, wga_ref, wgb_ref, wgc_ref, wpa_ref, wpb_ref, wpc_ref, o_ref):
    u = u_ref[...]
    acc = jax.nn.sigmoid(_dot(u, wga_ref[...])) * _dot(ya_ref[...], wpa_ref[...])
    acc = acc + jax.nn.sigmoid(_dot(u, wgb_ref[...])) * _dot(yb_ref[...], wpb_ref[...])
    acc = acc + jax.nn.sigmoid(_dot(u, wgc_ref[...])) * _dot(yc_ref[...], wpc_ref[...])
    o_ref[...] = acc.astype(BF16)


def _merge_call(u, ya, ybn, yc, w_main, w_pa, w_pb, w_pc, off_g, tm, tn):
    m, d = u.shape
    og = off_g // tn
    nd = d // tn
    rowspec = lambda a: pl.BlockSpec((tm, a.shape[1]), lambda i, j: (i, 0))
    gspec = lambda k: pl.BlockSpec((d, tn), lambda i, j, k=k: (0, og + k * nd + j))
    pspec = lambda w: pl.BlockSpec((w.shape[0], tn), lambda i, j: (0, j))
    return pl.pallas_call(
        _merge_kernel,
        grid=(m // tm, nd),
        in_specs=[rowspec(u), rowspec(ya), rowspec(ybn), rowspec(yc),
                  gspec(0), gspec(1), gspec(2), pspec(w_pa), pspec(w_pb), pspec(w_pc)],
        out_specs=pl.BlockSpec((tm, tn), lambda i, j: (i, j)),
        out_shape=jax.ShapeDtypeStruct((m, d), BF16),
        compiler_params=_params("arbitrary", "arbitrary"),
        name="merge",
    )(u, ya, ybn, yc, w_main, w_main, w_main, w_pa, w_pb, w_pc)


def _final_kernel(x_ref, m_ref, wo_ref, g_ref, mod_ref, o_ref):
    d = x_ref.shape[-1]
    r = _dot(m_ref[...], wo_ref[...])
    r = r * lax.rsqrt(jnp.mean(r * r, axis=-1, keepdims=True) + EPS) * g_ref[...]
    o_ref[...] = x_ref[...] + mod_ref[:, 2 * d:3 * d] * r


def _final_call(x, mm, w_o, g_post, mod3, mod_row, tm):
    m, d = x.shape
    return pl.pallas_call(
        _final_kernel,
        grid=(m // tm,),
        in_specs=[pl.BlockSpec((tm, d), lambda i: (i, 0)),
                  pl.BlockSpec((tm, d), lambda i: (i, 0)),
                  pl.BlockSpec((d, d), lambda i: (0, 0)),
                  pl.BlockSpec((1, d), lambda i: (0, 0)),
                  pl.BlockSpec((None, 1, 3 * d), lambda i: (mod_row(i, tm), 0, 0))],
        out_specs=pl.BlockSpec((tm, d), lambda i: (i, 0)),
        out_shape=jax.ShapeDtypeStruct((m, d), F32),
        compiler_params=_params("arbitrary"),
        name="final",
    )(x, mm, w_o, g_post.reshape(1, d), mod3)


def _tile(pref, *dims):
    t = pref
    while any(dim % t for dim in dims):
        t //= 2
    return t


def kernel(x_prompt, x_sample, state_ssd, c, c_ctx, w_mod, b_mod, g_pre, g_post, w_in, conv_a_w, ssd_conv_w,
           ssd_conv_b, dt_bias, a_log, d_skip, ssd_norm_g, conf_conv_w, conf_conv_b, conf_ln_g, conf_ln_b,
           w_pa, w_pb, w_pc, w_o):
    nbp, seq, d = x_prompt.shape
    nbs, dseq, _ = x_sample.shape
    depth = w_in.shape[0]
    heads, hd, ns = state_ssd.shape[3:]
    wb = heads * hd
    xbc_w = ssd_conv_w.shape[-1]
    gn = (xbc_w - wb) // 2
    n_groups = gn // ns
    hpg = heads // n_groups
    wa = conv_a_w.shape[-1]
    wc = conf_conv_w.shape[-1]
    mp, ms = nbp * seq, nbs * dseq
    m = mp + ms
    dims = (n_groups, hpg, hd, ns)

    o_ab, o_ac, o_ah, o_az = 0, wa, 2 * wa, 3 * wa
    o_bz = 4 * wa
    o_xbc = o_bz + wb
    o_dt = o_xbc + xbc_w
    o_cglu = o_dt + 2 * heads
    w_main = jnp.concatenate([w_in[:, :, :o_dt], w_in[:, :, o_cglu:]], axis=-1).astype(BF16)
    w_dt = w_in[:, :, o_dt:o_cglu].astype(BF16)
    o_ca = o_dt
    o_cg = o_ca + wc
    o_cz = o_cg + wc
    o_g = o_cz + wc

    tm = _tile(1024, mp, ms, dseq)
    tc = _tile(512, wa, wb, wc, gn, d)
    rb = _tile(256, tm)
    assert rb % seq == 0 and rb % GRID_W == 0 and tm % seq == 0
    seg_info = (mp // tm, seq, GRID_W, rb)

    def mod_row(i, t):
        return jnp.where(i < mp // t, 0, 1 + (i * t - mp) // dseq)

    rows = 8 * ((1 + nbs + 7) // 8)
    cvec = jnp.zeros((rows, d), F32).at[0].set(c_ctx).at[1:1 + nbs].set(c)
    mod_all = _mod_call(cvec, w_mod, b_mod)

    expand = (lax.broadcasted_iota(jnp.int32, (2 * heads, wb), 1) // hd
              == lax.broadcasted_iota(jnp.int32, (2 * heads, wb), 0)).astype(BF16)

    bt_p = seq
    bt_s = _tile(512, dseq)
    tmn = _tile(256, mp, ms)
    tmm = _tile(512, m)

    x = jnp.concatenate([x_prompt.reshape(mp, d), x_sample.reshape(ms, d)], axis=0)
    new_states = []
    for l in range(depth):
        mod3 = mod_all[l].reshape(rows, 1, 3 * d)
        wl = w_main[l]
        u = _premod_call(x, g_pre[l], mod3, mod_row, tm)
        ya = _mix_a_call(u, wl, conv_a_w[l], (o_ab, o_ac, o_ah, o_az), wa, tm, tc, seg_info)
        yc = _mix_c_call(u, wl, conf_conv_w[l], conf_conv_b[l], conf_ln_g[l], conf_ln_b[l],
                         (o_ca, o_cg, o_cz), wc, tm, tc, seg_info)
        xbc = _xbc_call(u, wl, ssd_conv_w[l], ssd_conv_b[l], o_xbc, xbc_w, tm, tc, seg_info)
        zs, dt = _zdt_call(u, wl, w_dt[l], dt_bias[l].reshape(-1), o_bz, wb, tm, tc)
        a_log2 = jnp.stack([a_log[l].reshape(-1), a_log[l, ::-1].reshape(-1)])
        dsk = jnp.repeat(d_skip[l], hd).reshape(1, wb)
        y_p, st = _ssd_call(xbc, dt, a_log2, dsk, expand, None, dims, 0, nbp, 1, bt_p, True)
        (y_s,) = _ssd_call(xbc, dt, a_log2, dsk, expand, state_ssd[:, l], dims, mp, nbs, dseq // bt_s, bt_s, False)
        new_states.append(st)
        ybn = jnp.concatenate([_ssdnorm_call(y_p, zs, ssd_norm_g[l], 0, tmn),
                               _ssdnorm_call(y_s, zs, ssd_norm_g[l], mp, tmn)], axis=0)
        mm = _merge_call(u, ya, ybn, yc, wl, w_pa[l].astype(BF16), w_pb[l].astype(BF16), w_pc[l].astype(BF16),
                         o_g, tmm, tc)
        x = _final_call(x, mm, w_o[l].astype(BF16), g_post[l], mod3, mod_row, tmm)
    new_state_ssd = jnp.stack(new_states, axis=1).astype(x_prompt.dtype)
    return (x[:mp].reshape(nbp, seq, d), x[mp:].reshape(nbs, dseq, d), new_state_ssd)
```

```python
import functools

import jax
import jax.numpy as jnp
from jax import lax
from jax.experimental import pallas as pl
from jax.experimental.pallas import tpu as pltpu

GRID_W = 64
SSD_CHUNK = 128
EPS = 1e-6
LANES = 128
V7X_VMEM_LIMIT_BYTES = 56 * 1024 * 1024

F32 = jnp.float32
BF16 = jnp.bfloat16


def _params(*sem):
    return pltpu.CompilerParams(dimension_semantics=sem, vmem_limit_bytes=V7X_VMEM_LIMIT_BYTES)


def _silu(x):
    return x * jax.nn.sigmoid(x)


def _dot(a, b):
    return jnp.dot(a, b, preferred_element_type=F32)


def _split3(x):
    hi = x.astype(BF16)
    r1 = x - hi.astype(F32)
    mid = r1.astype(BF16)
    lo = (r1 - mid.astype(F32)).astype(BF16)
    return hi, mid, lo


def _dot_exact_lhs(x, sel):
    hi, mid, lo = _split3(x)
    return _dot(hi, sel) + _dot(mid, sel) + _dot(lo, sel)


def _dot_exact_rhs(sel, x):
    hi, mid, lo = _split3(x)
    return _dot(sel, hi) + _dot(sel, mid) + _dot(sel, lo)


def _mod_kernel(c_ref, w_ref, b_ref, o_ref):
    s = _silu(c_ref[...]).astype(BF16)
    o_ref[...] = _dot(s, w_ref[...].astype(BF16)) + b_ref[...]


def _mod_call(cvec, w_mod, b_mod):
    depth, d, n3 = w_mod.shape
    rows = cvec.shape[0]
    tn = min(1024, n3)
    return pl.pallas_call(
        _mod_kernel,
        grid=(depth, n3 // tn),
        in_specs=[pl.BlockSpec((rows, d), lambda l, j: (0, 0)),
                  pl.BlockSpec((None, d, tn), lambda l, j: (l, 0, j)),
                  pl.BlockSpec((None, 1, tn), lambda l, j: (l, 0, j))],
        out_specs=pl.BlockSpec((None, rows, tn), lambda l, j: (l, 0, j)),
        out_shape=jax.ShapeDtypeStruct((depth, rows, n3), F32),
        compiler_params=_params("arbitrary", "arbitrary"),
        name="mod",
    )(cvec, w_mod, b_mod.reshape(depth, 1, n3))


def _premod_kernel(x_ref, g_ref, mod_ref, u_ref):
    x = x_ref[...]
    d = x.shape[-1]
    y = x * lax.rsqrt(jnp.mean(x * x, axis=-1, keepdims=True) + EPS) * g_ref[...]
    u_ref[...] = (y * (1.0 + mod_ref[:, d:2 * d]) + mod_ref[:, 0:d]).astype(BF16)


def _premod_call(x, g_pre, mod3, mod_row, tm):
    m, d = x.shape
    return pl.pallas_call(
        _premod_kernel,
        grid=(m // tm,),
        in_specs=[pl.BlockSpec((tm, d), lambda i: (i, 0)),
                  pl.BlockSpec((1, d), lambda i: (0, 0)),
                  pl.BlockSpec((None, 1, 3 * d), lambda i: (mod_row(i, tm), 0, 0))],
        out_specs=pl.BlockSpec((tm, d), lambda i: (i, 0)),
        out_shape=jax.ShapeDtypeStruct((m, d), BF16),
        compiler_params=_params("arbitrary"),
        name="premod",
    )(x, g_pre.reshape(1, d), mod3)


CONV_SUB = GRID_W


def _halo_rows(taps):
    return 8 * ((taps // 2 + 7) // 8)


def _pad_shape(n, tc, halo):
    return (tc // LANES, (n // CONV_SUB) * (CONV_SUB + 2 * halo), LANES)


def _conv_stage(pad_ref, val, seg, halo):
    n, tc = val.shape
    stride = CONV_SUB + 2 * halo
    zeros = jnp.zeros((halo, LANES), F32)
    nsub = n // CONV_SUB
    for b in range(nsub):
        r = b * CONV_SUB
        base = b * stride
        for c in range(tc // LANES):
            lanes = slice(c * LANES, (c + 1) * LANES)
            before = zeros if b == 0 else jnp.where((r & (seg - 1)) == 0, zeros, val[r - halo:r, lanes])
            after = zeros if b == nsub - 1 else jnp.where(((r + CONV_SUB) & (seg - 1)) == 0, zeros,
                                                          val[r + CONV_SUB:r + CONV_SUB + halo, lanes])
            pad_ref[c, base:base + halo, :] = before
            pad_ref[c, base + halo:base + halo + CONV_SUB, :] = val[r:r + CONV_SUB, lanes]
            pad_ref[c, base + halo + CONV_SUB:base + stride, :] = after


def _conv_taps(pad_ref, w_ref, n, halo, emit):
    taps = w_ref.shape[0]
    stride = CONV_SUB + 2 * halo
    for b in range(n // CONV_SUB):
        for c in range(pad_ref.shape[0]):
            c0 = c * LANES
            acc = None
            for k in range(taps):
                top = b * stride + halo + k - taps // 2
                term = pad_ref[c, top:top + CONV_SUB, :] * w_ref[k:k + 1, c0:c0 + LANES]
                acc = term if acc is None else acc + term
            emit(b * CONV_SUB, c0, acc)


def _mix_a_kernel(n_prompt_tiles, seg_p, seg_s, u_ref, wb_ref, wc_ref, wh_ref, wz_ref, cw_ref, o_ref, pad_scr):
    i = pl.program_id(0)
    seg = jnp.where(i < n_prompt_tiles, seg_p, seg_s)
    halo = _halo_rows(cw_ref.shape[0])
    hr = u_ref.shape[0] // 2
    for hf in range(2):
        uh = u_ref[hf * hr:(hf + 1) * hr, :]
        _conv_stage(pad_scr.at[hf], _dot(uh, wc_ref[...]) * _dot(uh, wh_ref[...]), seg, halo)
        bz = _dot(uh, wb_ref[...]) * _silu(_dot(uh, wz_ref[...]))

        def emit(r0, c0, acc, hf=hf, bz=bz):
            o_ref[hf * hr + r0:hf * hr + r0 + CONV_SUB, c0:c0 + LANES] = (
                acc * bz[r0:r0 + CONV_SUB, c0:c0 + LANES]).astype(BF16)

        _conv_taps(pad_scr.at[hf], cw_ref, hr, halo, emit)


def _mix_a_call(u, w_main, conv_w, offs, wa, tm, tc, seg_info):
    m, d = u.shape
    ob, oc, oh, oz = (o // tc for o in offs)
    wspec = lambda o: pl.BlockSpec((d, tc), lambda i, j, o=o: (0, o + j))
    halo = _halo_rows(conv_w.shape[0])
    return pl.pallas_call(
        functools.partial(_mix_a_kernel, *seg_info),
        grid=(m // tm, wa // tc),
        in_specs=[pl.BlockSpec((tm, d), lambda i, j: (i, 0)),
                  wspec(ob), wspec(oc), wspec(oh), wspec(oz),
                  pl.BlockSpec((conv_w.shape[0], tc), lambda i, j: (0, j))],
        out_specs=pl.BlockSpec((tm, tc), lambda i, j: (i, j)),
        out_shape=jax.ShapeDtypeStruct((m, wa), BF16),
        scratch_shapes=[pltpu.VMEM((2,) + _pad_shape(tm // 2, tc, halo), F32)],
        compiler_params=_params("arbitrary", "arbitrary"),
        name="mix_a",
    )(u, w_main, w_main, w_main, w_main, conv_w)


def _mix_c_kernel(n_prompt_tiles, seg_p, seg_s, ln_rows, u_ref, wa_ref, wg_ref, wz_ref, cw_ref,
                  cb_ref, lg_ref, lb_ref, o_ref, pad_scr, conv_scr, zs_scr):
    i = pl.program_id(0)
    j = pl.program_id(1)
    nj = pl.num_programs(1)
    seg = jnp.where(i < n_prompt_tiles, seg_p, seg_s)
    n_tc, tm, _ = conv_scr.shape
    halo = _halo_rows(cw_ref.shape[0])
    hr = tm // 2
    for hf in range(2):
        rows = slice(hf * hr, (hf + 1) * hr)
        uh = u_ref[rows, :]
        _conv_stage(pad_scr.at[hf], _dot(uh, wa_ref[...]) * jax.nn.sigmoid(_dot(uh, wg_ref[...])), seg, halo)
        zs_scr[j, rows, :] = _silu(_dot(uh, wz_ref[...])).astype(BF16)

        def emit(r0, c0, acc, hf=hf):
            conv_scr[j, hf * hr + r0:hf * hr + r0 + CONV_SUB, c0:c0 + LANES] = acc + cb_ref[:, c0:c0 + LANES]

        _conv_taps(pad_scr.at[hf], cw_ref, hr, halo, emit)

    @pl.when(j == nj - 1)
    def _():
        def body(b, carry):
            r0 = pl.multiple_of(b * ln_rows, ln_rows)
            v = jnp.concatenate([conv_scr[t, pl.ds(r0, ln_rows), :] for t in range(n_tc)], axis=1)
            zs = jnp.concatenate([zs_scr[t, pl.ds(r0, ln_rows), :] for t in range(n_tc)], axis=1)
            mu = jnp.mean(v, axis=-1, keepdims=True)
            cen = v - mu
            var = jnp.mean(cen * cen, axis=-1, keepdims=True)
            y = cen * lax.rsqrt(var + EPS) * lg_ref[...] + lb_ref[...]
            o_ref[pl.ds(r0, ln_rows), :] = (_silu(y) * zs.astype(F32)).astype(BF16)
            return carry

        lax.fori_loop(0, tm // ln_rows, body, 0)


def _mix_c_call(u, w_main, conv_w, conv_b, ln_g, ln_b, offs, wc, tm, tc, seg_info):
    m, d = u.shape
    oa, og, oz = (o // tc for o in offs)
    wspec = lambda o: pl.BlockSpec((d, tc), lambda i, j, o=o: (0, o + j))
    halo = _halo_rows(conv_w.shape[0])
    full = lambda shape: pl.BlockSpec(shape, lambda i, j: (0, 0))
    return pl.pallas_call(
        functools.partial(_mix_c_kernel, *seg_info, 16),
        grid=(m // tm, wc // tc),
        in_specs=[pl.BlockSpec((tm, d), lambda i, j: (i, 0)),
                  wspec(oa), wspec(og), wspec(oz),
                  pl.BlockSpec((conv_w.shape[0], tc), lambda i, j: (0, j)),
                  pl.BlockSpec((1, tc), lambda i, j: (0, j)),
                  full((1, wc)), full((1, wc))],
        out_specs=pl.BlockSpec((tm, wc), lambda i, j: (i, 0)),
        out_shape=jax.ShapeDtypeStruct((m, wc), BF16),
        scratch_shapes=[pltpu.VMEM((2,) + _pad_shape(tm // 2, tc, halo), F32),
                        pltpu.VMEM((wc // tc, tm, tc), F32), pltpu.VMEM((wc // tc, tm, tc), BF16)],
        compiler_params=_params("arbitrary", "arbitrary"),
        name="mix_c",
    )(u, w_main, w_main, w_main, conv_w, conv_b.reshape(1, wc), ln_g.reshape(1, wc), ln_b.reshape(1, wc))


def _xbc_kernel(n_prompt_tiles, seg_p, seg_s, u_ref, w_ref, cw_ref, cb_ref, o_ref, pad_scr):
    i = pl.program_id(0)
    seg = jnp.where(i < n_prompt_tiles, seg_p, seg_s)
    halo = _halo_rows(cw_ref.shape[0])
    hr = u_ref.shape[0] // 2
    for hf in range(2):
        _conv_stage(pad_scr.at[hf], _dot(u_ref[hf * hr:(hf + 1) * hr, :], w_ref[...]), seg, halo)

        def emit(r0, c0, acc, hf=hf):
            o_ref[hf * hr + r0:hf * hr + r0 + CONV_SUB, c0:c0 + LANES] = _silu(
                acc + cb_ref[:, c0:c0 + LANES]).astype(BF16)

        _conv_taps(pad_scr.at[hf], cw_ref, hr, halo, emit)


def _xbc_call(u, w_main, conv_w, conv_b, off, width, tm, tc, seg_info):
    m, d = u.shape
    o = off // tc
    halo = _halo_rows(conv_w.shape[0])
    return pl.pallas_call(
        functools.partial(_xbc_kernel, *seg_info),
        grid=(m // tm, width // tc),
        in_specs=[pl.BlockSpec((tm, d), lambda i, j: (i, 0)),
                  pl.BlockSpec((d, tc), lambda i, j: (0, o + j)),
                  pl.BlockSpec((conv_w.shape[0], tc), lambda i, j: (0, j)),
                  pl.BlockSpec((1, tc), lambda i, j: (0, j))],
        out_specs=pl.BlockSpec((tm, tc), lambda i, j: (i, j)),
        out_shape=jax.ShapeDtypeStruct((m, width), BF16),
        scratch_shapes=[pltpu.VMEM((2,) + _pad_shape(tm // 2, tc, halo), F32)],
        compiler_params=_params("arbitrary", "arbitrary"),
        name="xbc",
    )(u, w_main, conv_w, conv_b.reshape(1, width))


def _zdt_kernel(u_ref, wz_ref, wdt_ref, dtb_ref, zs_ref, dt_ref):
    u = u_ref[...]
    zs_ref[...] = _silu(_dot(u, wz_ref[...])).astype(BF16)

    @pl.when(pl.program_id(1) == 0)
    def _():
        v = _dot(u, wdt_ref[...]) + dtb_ref[...]
        dt_ref[...] = jnp.maximum(v, 0.0) + jnp.log1p(jnp.exp(-jnp.abs(v)))


def _zdt_call(u, w_main, w_dt, dt_bias, off, width, tm, tc):
    m, d = u.shape
    o = off // tc
    nh2 = w_dt.shape[-1]
    return pl.pallas_call(
        _zdt_kernel,
        grid=(m // tm, width // tc),
        in_specs=[pl.BlockSpec((tm, d), lambda i, j: (i, 0)),
                  pl.BlockSpec((d, tc), lambda i, j: (0, o + j)),
                  pl.BlockSpec((d, nh2), lambda i, j: (0, 0)),
                  pl.BlockSpec((1, nh2), lambda i, j: (0, 0))],
        out_specs=[pl.BlockSpec((tm, tc), lambda i, j: (i, j)),
                   pl.BlockSpec((tm, nh2), lambda i, j: (i, 0))],
        out_shape=[jax.ShapeDtypeStruct((m, width), BF16), jax.ShapeDtypeStruct((m, nh2), F32)],
        compiler_params=_params("arbitrary", "arbitrary"),
        name="zdt",
    )(u, w_main, w_dt, dt_bias.reshape(1, nh2))


def _ssd_kernel(dims, has_h0, emit_state, *refs):
    n_groups, hpg, hd, ns = dims
    refs = list(refs)
    x_ref, bm_ref, cm_ref, dt_ref, alog_ref, dsk_ref, exp_ref = refs[:7]
    refs = refs[7:]
    h0_ref = refs.pop(0) if has_h0 else None
    y_ref = refs.pop(0)
    so_ref = refs.pop(0) if emit_state else None
    state = refs.pop(0)

    d = pl.program_id(1)
    b = pl.program_id(2)
    nb = pl.num_programs(2)
    is_f = d == 0
    bt = x_ref.shape[0]
    q = SSD_CHUNK
    nck = bt // q
    gw = hpg * hd
    pairs = gw // LANES

    @pl.when(b == 0)
    def _():
        if has_h0:
            for g in range(n_groups):
                hg = h0_ref[g * hpg:(g + 1) * hpg].reshape(gw, ns)
                state[g] = hg.T
        else:
            state[...] = jnp.zeros_like(state)

    ii = lax.broadcasted_iota(jnp.int32, (q, q), 0)
    jj = lax.broadcasted_iota(jnp.int32, (q, q), 1)
    sgn = jnp.where(is_f, 1, -1)
    tri = (ii - jj) * sgn >= 0
    cum_mat = jnp.where(tri, 1.0, 0.0).astype(BF16)
    lane = lax.broadcasted_iota(jnp.int32, (q, LANES), 1)
    keep_lo = jnp.where(lane < hd, 1.0, 0.0).astype(BF16)
    keep_hi = jnp.where(lane < hd, 0.0, 1.0).astype(BF16)
    a_sel = -jnp.exp(alog_ref[pl.ds(d, 1), :])
    dsk = dsk_ref[...] * jnp.where(is_f, 1.0, 0.0)

    def chunk(k, carry):
        c = jnp.where(is_f, k, nck - 1 - k)
        r0 = pl.multiple_of(c * q, q)
        rows = pl.ds(r0, q)
        dt_c = dt_ref[rows, :]
        dt_sel = jnp.where(is_f, dt_c, pltpu.roll(dt_c, dt_c.shape[-1] // 2, 1))
        acum = _dot_exact_rhs(cum_mat, dt_sel * a_sel)
        total = jnp.where(is_f, acum[q - 1:q, :], acum[0:1, :])
        w_end = jnp.exp(total - acum) * dt_sel
        acum_t = acum.T
        dt_t = dt_sel.T
        w_end_t = w_end.T
        dec_all = jnp.exp(_dot_exact_lhs(jnp.broadcast_to(total, (8, total.shape[-1])), exp_ref[...])[0:1, :])

        for g in range(n_groups):
            cm_g = cm_ref[rows, g * ns:(g + 1) * ns]
            bm_g = bm_ref[rows, g * ns:(g + 1) * ns]
            scores = lax.dot_general(cm_g, bm_g, (((1,), (1,)), ((), ())), preferred_element_type=F32)
            cm_f = cm_g.astype(F32)
            bm_t = bm_g.astype(F32).T
            for p in range(pairs):
                lanes = slice(g * gw + p * LANES, g * gw + (p + 1) * LANES)
                x_pair = x_ref[rows, lanes]
                x_lo = x_pair * keep_lo
                x_hi = x_pair * keep_hi
                h_pair = state[g, :, p * LANES:(p + 1) * LANES]
                h_bf = h_pair.astype(BF16)
                h_lo = h_bf * keep_lo
                h_hi = h_bf * keep_hi
                m_parts, e_parts, w_parts = [], [], []
                for hh in range(2):
                    h = g * hpg + p * (LANES // hd) + hh
                    colb = jnp.broadcast_to(acum[:, h:h + 1], (q, q))
                    decay = jnp.exp(jnp.where(tri, colb - acum_t[h:h + 1, :], -jnp.inf))
                    m_parts.append((scores * decay * dt_t[h:h + 1, :]).astype(BF16))
                    e_parts.append((jnp.exp(colb) * cm_f).astype(BF16))
                    w_parts.append((bm_t * w_end_t[h:h + 1, :]).astype(BF16))
                lhs_y = jnp.concatenate(m_parts + e_parts, axis=1)
                rhs_y = jnp.concatenate([x_lo, x_hi, h_lo, h_hi], axis=0)
                y = _dot(lhs_y, rhs_y)
                s_new = _dot(jnp.concatenate(w_parts, axis=1), jnp.concatenate([x_lo, x_hi], axis=0))
                state[g, :, p * LANES:(p + 1) * LANES] = h_pair * dec_all[:, lanes] + s_new
                y_ref[rows, lanes] = y + x_pair.astype(F32) * dsk[:, lanes]
        return carry

    lax.fori_loop(0, nck, chunk, 0)

    if emit_state:
        @pl.when(b == nb - 1)
        def _():
            for g in range(n_groups):
                so_ref[g * hpg:(g + 1) * hpg] = state[g].T.reshape(hpg, hd, ns)


def _ssd_call(xbc, dt, a_log2, dsk, expand, h0, dims, row0, nseq, nblk, bt, emit_state):
    n_groups, hpg, hd, ns = dims
    heads = n_groups * hpg
    wb = heads * hd
    gn = n_groups * ns
    m = xbc.shape[0]
    blk0 = row0 // bt

    def rowblk(s, d, b):
        return blk0 + s * nblk + b + d * (nblk - 1 - 2 * b)

    in_specs = [pl.BlockSpec((bt, wb), lambda s, d, b: (rowblk(s, d, b), 0)),
                pl.BlockSpec((bt, gn), lambda s, d, b: (rowblk(s, d, b), wb // gn)),
                pl.BlockSpec((bt, gn), lambda s, d, b: (rowblk(s, d, b), wb // gn + 1)),
                pl.BlockSpec((bt, 2 * heads), lambda s, d, b: (rowblk(s, d, b), 0)),
                pl.BlockSpec((2, 2 * heads), lambda s, d, b: (0, 0)),
                pl.BlockSpec((1, wb), lambda s, d, b: (0, 0)),
                pl.BlockSpec(expand.shape, lambda s, d, b: (0, 0))]
    args = [xbc, xbc, xbc, dt, a_log2, dsk, expand]
    if h0 is not None:
        in_specs.append(pl.BlockSpec((None, None, heads, hd, ns), lambda s, d, b: (s, d, 0, 0, 0)))
        args.append(h0)
    nrows = nseq * nblk * bt
    out_specs = [pl.BlockSpec((None, bt, wb), lambda s, d, b: (d, rowblk(s, d, b) - blk0, 0))]
    out_shape = [jax.ShapeDtypeStruct((2, nrows, wb), F32)]
    if emit_state:
        out_specs.append(pl.BlockSpec((None, None, heads, hd, ns), lambda s, d, b: (s, d, 0, 0, 0)))
        out_shape.append(jax.ShapeDtypeStruct((nseq, 2, heads, hd, ns), F32))
    return pl.pallas_call(
        functools.partial(_ssd_kernel, dims, h0 is not None, emit_state),
        grid=(nseq, 2, nblk),
        in_specs=in_specs,
        out_specs=out_specs,
        out_shape=out_shape,
        scratch_shapes=[pltpu.VMEM((n_groups, ns, hpg * hd), F32)],
        compiler_params=_params("arbitrary", "arbitrary", "arbitrary"),
        name="ssd",
    )(*args)


def _ssdnorm_kernel(rows, y_ref, zs_ref, g_ref, o_ref):
    tm = o_ref.shape[0]

    def body(b, carry):
        r0 = pl.multiple_of(b * rows, rows)
        v = (y_ref[0, pl.ds(r0, rows), :] + y_ref[1, pl.ds(r0, rows), :]) * zs_ref[pl.ds(r0, rows), :].astype(F32)
        o_ref[pl.ds(r0, rows), :] = (v * lax.rsqrt(jnp.mean(v * v, axis=-1, keepdims=True) + EPS)
                                     * g_ref[...]).astype(BF16)
        return carry

    lax.fori_loop(0, tm // rows, body, 0)


def _ssdnorm_call(y2, zs, g, row0, tm):
    _, mrows, wb = y2.shape
    blk0 = row0 // tm
    return pl.pallas_call(
        functools.partial(_ssdnorm_kernel, 16),
        grid=(mrows // tm,),
        in_specs=[pl.BlockSpec((2, tm, wb), lambda i: (0, i, 0)),
                  pl.BlockSpec((tm, wb), lambda i: (blk0 + i, 0)),
                  pl.BlockSpec((1, wb), lambda i: (0, 0))],
        out_specs=pl.BlockSpec((tm, wb), lambda i: (i, 0)),
        out_shape=jax.ShapeDtypeStruct((mrows, wb), BF16),
        compiler_params=_params("arbitrary"),
        name="ssdnorm",
    )(y2, zs, g.reshape(1, wb))


def _merge_kernel(u_ref, ya_ref, yb_ref, yc_ref, wga_ref, wgb_ref, wgc_ref, wpa_ref, wpb_ref, wpc_ref, o_ref):
    u = u_ref[...]
    acc = jax.nn.sigmoid(_dot(u, wga_ref[...])) * _dot(ya_ref[...], wpa_ref[...])
    acc = acc + jax.nn.sigmoid(_dot(u, wgb_ref[...])) * _dot(yb_ref[...], wpb_ref[...])
    acc = acc + jax.nn.sigmoid(_dot(u, wgc_ref[...])) * _dot(yc_ref[...], wpc_ref[...])
    o_ref[...] = acc.astype(BF16)


def _merge_call(u, ya, ybn, yc, w_main, w_pa, w_pb, w_pc, off_g, tm, tn):
    m, d = u.shape
    og = off_g // tn
    nd = d // tn
    rowspec = lambda a: pl.BlockSpec((tm, a.shape[1]), lambda i, j: (i, 0))
    gspec = lambda k: pl.BlockSpec((d, tn), lambda i, j, k=k: (0, og + k * nd + j))
    pspec = lambda w: pl.BlockSpec((w.shape[0], tn), lambda i, j: (0, j))
    return pl.pallas_call(
        _merge_kernel,
        grid=(m // tm, nd),
        in_specs=[rowspec(u), rowspec(ya), rowspec(ybn), rowspec(yc),
                  gspec(0), gspec(1), gspec(2), pspec(w_pa), pspec(w_pb), pspec(w_pc)],
        out_specs=pl.BlockSpec((tm, tn), lambda i, j: (i, j)),
        out_shape=jax.ShapeDtypeStruct((m, d), BF16),
        compiler_params=_params("arbitrary", "arbitrary"),
        name="merge",
    )(u, ya, ybn, yc, w_main, w_main, w_main, w_pa, w_pb, w_pc)


def _final_kernel(x_ref, m_ref, wo_ref, g_ref, mod_ref, o_ref):
    d = x_ref.shape[-1]
    r = _dot(m_ref[...], wo_ref[...])
    r = r * lax.rsqrt(jnp.mean(r * r, axis=-1, keepdims=True) + EPS) * g_ref[...]
    o_ref[...] = x_ref[...] + mod_ref[:, 2 * d:3 * d] * r


def _final_call(x, mm, w_o, g_post, mod3, mod_row, tm):
    m, d = x.shape
    return pl.pallas_call(
        _final_kernel,
        grid=(m // tm,),
        in_specs=[pl.BlockSpec((tm, d), lambda i: (i, 0)),
                  pl.BlockSpec((tm, d), lambda i: (i, 0)),
                  pl.BlockSpec((d, d), lambda i: (0, 0)),
                  pl.BlockSpec((1, d), lambda i: (0, 0)),
                  pl.BlockSpec((None, 1, 3 * d), lambda i: (mod_row(i, tm), 0, 0))],
        out_specs=pl.BlockSpec((tm, d), lambda i: (i, 0)),
        out_shape=jax.ShapeDtypeStruct((m, d), F32),
        compiler_params=_params("arbitrary"),
        name="final",
    )(x, mm, w_o, g_post.reshape(1, d), mod3)


def _tile(pref, *dims):
    t = pref
    while any(dim % t for dim in dims):
        t //= 2
    return t


def kernel(x_prompt, x_sample, state_ssd, c, c_ctx, w_mod, b_mod, g_pre, g_post, w_in, conv_a_w, ssd_conv_w,
           ssd_conv_b, dt_bias, a_log, d_skip, ssd_norm_g, conf_conv_w, conf_conv_b, conf_ln_g, conf_ln_b,
           w_pa, w_pb, w_pc, w_o):
    nbp, seq, d = x_prompt.shape
    nbs, dseq, _ = x_sample.shape
    depth = w_in.shape[0]
    heads, hd, ns = state_ssd.shape[3:]
    wb = heads * hd
    xbc_w = ssd_conv_w.shape[-1]
    gn = (xbc_w - wb) // 2
    n_groups = gn // ns
    hpg = heads // n_groups
    wa = conv_a_w.shape[-1]
    wc = conf_conv_w.shape[-1]
    mp, ms = nbp * seq, nbs * dseq
    m = mp + ms
    dims = (n_groups, hpg, hd, ns)

    o_ab, o_ac, o_ah, o_az = 0, wa, 2 * wa, 3 * wa
    o_bz = 4 * wa
    o_xbc = o_bz + wb
    o_dt = o_xbc + xbc_w
    o_cglu = o_dt + 2 * heads
    w_main = jnp.concatenate([w_in[:, :, :o_dt], w_in[:, :, o_cglu:]], axis=-1).astype(BF16)
    w_dt = w_in[:, :, o_dt:o_cglu].astype(BF16)
    o_ca = o_dt
    o_cg = o_ca + wc
    o_cz = o_cg + wc
    o_g = o_cz + wc

    tm = _tile(1024, mp, ms, dseq)
    tc = _tile(512, wa, wb, wc, gn, d)
    assert (tm // 2) % seq == 0 and seq % CONV_SUB == 0 and seq & (seq - 1) == 0
    seg_info = (mp // tm, seq, GRID_W)

    def mod_row(i, t):
        return jnp.where(i < mp // t, 0, 1 + (i * t - mp) // dseq)

    rows = 8 * ((1 + nbs + 7) // 8)
    cvec = jnp.zeros((rows, d), F32).at[0].set(c_ctx).at[1:1 + nbs].set(c)
    mod_all = _mod_call(cvec, w_mod, b_mod)

    expand = (lax.broadcasted_iota(jnp.int32, (2 * heads, wb), 1) // hd
              == lax.broadcasted_iota(jnp.int32, (2 * heads, wb), 0)).astype(BF16)

    bt_p = seq
    bt_s = _tile(512, dseq)
    tmn = _tile(256, mp, ms)
    tmm = _tile(512, m)

    x = jnp.concatenate([x_prompt.reshape(mp, d), x_sample.reshape(ms, d)], axis=0)
    new_states = []
    for l in range(depth):
        mod3 = mod_all[l].reshape(rows, 1, 3 * d)
        wl = w_main[l]
        u = _premod_call(x, g_pre[l], mod3, mod_row, tm)
        ya = _mix_a_call(u, wl, conv_a_w[l], (o_ab, o_ac, o_ah, o_az), wa, tm, tc, seg_info)
        yc = _mix_c_call(u, wl, conf_conv_w[l], conf_conv_b[l], conf_ln_g[l], conf_ln_b[l],
                         (o_ca, o_cg, o_cz), wc, tm, tc, seg_info)
        xbc = _xbc_call(u, wl, ssd_conv_w[l], ssd_conv_b[l], o_xbc, xbc_w, tm, tc, seg_info)
        zs, dt = _zdt_call(u, wl, w_dt[l], dt_bias[l].reshape(-1), o_bz, wb, tm, tc)
        a_log2 = jnp.stack([a_log[l].reshape(-1), a_log[l, ::-1].reshape(-1)])
        dsk = jnp.repeat(d_skip[l], hd).reshape(1, wb)
        y_p, st = _ssd_call(xbc, dt, a_log2, dsk, expand, None, dims, 0, nbp, 1, bt_p, True)
        (y_s,) = _ssd_call(xbc, dt, a_log2, dsk, expand, state_ssd[:, l], dims, mp, nbs, dseq // bt_s, bt_s, False)
        new_states.append(st)
        ybn = jnp.concatenate([_ssdnorm_call(y_p, zs, ssd_norm_g[l], 0, tmn),
                               _ssdnorm_call(y_s, zs, ssd_norm_g[l], mp, tmn)], axis=0)
        mm = _merge_call(u, ya, ybn, yc, wl, w_pa[l].astype(BF16), w_pb[l].astype(BF16), w_pc[l].astype(BF16),
                         o_g, tmm, tc)
        x = _final_call(x, mm, w_o[l].astype(BF16), g_post[l], mod3, mod_row, tmm)
    new_state_ssd = jnp.stack(new_states, axis=1).astype(x_prompt.dtype)
    return (x[:mp].reshape(nbp, seq, d), x[mp:].reshape(nbs, dseq, d), new_state_ssd)
```

```python
import functools

import jax
import jax.numpy as jnp
from jax import lax
from jax.experimental import pallas as pl
from jax.experimental.pallas import tpu as pltpu

GRID_W = 64
SSD_CHUNK = 128
EPS = 1e-6
LANES = 128
V7X_VMEM_LIMIT_BYTES = 56 * 1024 * 1024

F32 = jnp.float32
BF16 = jnp.bfloat16


def _params(*sem):
    return pltpu.CompilerParams(dimension_semantics=sem, vmem_limit_bytes=V7X_VMEM_LIMIT_BYTES)


def _silu(x):
    return x * jax.nn.sigmoid(x)


def _dot(a, b):
    return jnp.dot(a, b, preferred_element_type=F32)


def _split3(x):
    hi = x.astype(BF16)
    r1 = x - hi.astype(F32)
    mid = r1.astype(BF16)
    lo = (r1 - mid.astype(F32)).astype(BF16)
    return hi, mid, lo


def _dot_exact_lhs(x, sel):
    hi, mid, lo = _split3(x)
    return _dot(hi, sel) + _dot(mid, sel) + _dot(lo, sel)


def _dot_exact_rhs(sel, x):
    hi, mid, lo = _split3(x)
    return _dot(sel, hi) + _dot(sel, mid) + _dot(sel, lo)


def _mod_kernel(c_ref, w_ref, b_ref, o_ref):
    s = _silu(c_ref[...]).astype(BF16)
    o_ref[...] = _dot(s, w_ref[...].astype(BF16)) + b_ref[...]


def _mod_call(cvec, w_mod, b_mod):
    depth, d, n3 = w_mod.shape
    rows = cvec.shape[0]
    tn = min(1024, n3)
    return pl.pallas_call(
        _mod_kernel,
        grid=(depth, n3 // tn),
        in_specs=[pl.BlockSpec((rows, d), lambda l, j: (0, 0)),
                  pl.BlockSpec((None, d, tn), lambda l, j: (l, 0, j)),
                  pl.BlockSpec((None, 1, tn), lambda l, j: (l, 0, j))],
        out_specs=pl.BlockSpec((None, rows, tn), lambda l, j: (l, 0, j)),
        out_shape=jax.ShapeDtypeStruct((depth, rows, n3), F32),
        compiler_params=_params("arbitrary", "arbitrary"),
        name="mod",
    )(cvec, w_mod, b_mod.reshape(depth, 1, n3))


def _premod_kernel(x_ref, g_ref, mod_ref, u_ref):
    x = x_ref[...]
    d = x.shape[-1]
    y = x * lax.rsqrt(jnp.mean(x * x, axis=-1, keepdims=True) + EPS) * g_ref[...]
    u_ref[...] = (y * (1.0 + mod_ref[:, d:2 * d]) + mod_ref[:, 0:d]).astype(BF16)


def _premod_call(x, g_pre, mod3, mod_row, tm):
    m, d = x.shape
    return pl.pallas_call(
        _premod_kernel,
        grid=(m // tm,),
        in_specs=[pl.BlockSpec((tm, d), lambda i: (i, 0)),
                  pl.BlockSpec((1, d), lambda i: (0, 0)),
                  pl.BlockSpec((None, 1, 3 * d), lambda i: (mod_row(i, tm), 0, 0))],
        out_specs=pl.BlockSpec((tm, d), lambda i: (i, 0)),
        out_shape=jax.ShapeDtypeStruct((m, d), BF16),
        compiler_params=_params("arbitrary"),
        name="premod",
    )(x, g_pre.reshape(1, d), mod3)


CONV_SUB = GRID_W


def _halo_rows(taps):
    return 8 * ((taps // 2 + 7) // 8)


def _pad_shape(n, tc, halo):
    return (tc // LANES, (n // CONV_SUB) * (CONV_SUB + 2 * halo), LANES)


def _conv_stage(pad_ref, val, seg, halo):
    n, tc = val.shape
    stride = CONV_SUB + 2 * halo
    zeros = jnp.zeros((halo, LANES), F32)
    nsub = n // CONV_SUB
    for b in range(nsub):
        r = b * CONV_SUB
        base = b * stride
        for c in range(tc // LANES):
            lanes = slice(c * LANES, (c + 1) * LANES)
            before = zeros if b == 0 else jnp.where((r & (seg - 1)) == 0, zeros, val[r - halo:r, lanes])
            after = zeros if b == nsub - 1 else jnp.where(((r + CONV_SUB) & (seg - 1)) == 0, zeros,
                                                          val[r + CONV_SUB:r + CONV_SUB + halo, lanes])
            pad_ref[c, base:base + halo, :] = before
            pad_ref[c, base + halo:base + halo + CONV_SUB, :] = val[r:r + CONV_SUB, lanes]
            pad_ref[c, base + halo + CONV_SUB:base + stride, :] = after


def _conv_taps(pad_ref, w_ref, n, halo, emit):
    taps = w_ref.shape[0]
    stride = CONV_SUB + 2 * halo
    for b in range(n // CONV_SUB):
        for c in range(pad_ref.shape[0]):
            c0 = c * LANES
            acc = None
            for k in range(taps):
                top = b * stride + halo + k - taps // 2
                term = pad_ref[c, top:top + CONV_SUB, :] * w_ref[k:k + 1, c0:c0 + LANES]
                acc = term if acc is None else acc + term
            emit(b * CONV_SUB, c0, acc)


def _mix_a_kernel(n_prompt_tiles, seg_p, seg_s, u_ref, wb_ref, wc_ref, wh_ref, wz_ref, cw_ref, o_ref, pad_scr):
    i = pl.program_id(0)
    seg = jnp.where(i < n_prompt_tiles, seg_p, seg_s)
    halo = _halo_rows(cw_ref.shape[0])
    hr = u_ref.shape[0] // 2
    for hf in range(2):
        uh = u_ref[hf * hr:(hf + 1) * hr, :]
        _conv_stage(pad_scr.at[hf], _dot(uh, wc_ref[...]) * _dot(uh, wh_ref[...]), seg, halo)
        bz = _dot(uh, wb_ref[...]) * _silu(_dot(uh, wz_ref[...]))

        def emit(r0, c0, acc, hf=hf, bz=bz):
            o_ref[hf * hr + r0:hf * hr + r0 + CONV_SUB, c0:c0 + LANES] = (
                acc * bz[r0:r0 + CONV_SUB, c0:c0 + LANES]).astype(BF16)

        _conv_taps(pad_scr.at[hf], cw_ref, hr, halo, emit)


def _mix_a_call(u, w_main, conv_w, offs, wa, tm, tc, seg_info):
    m, d = u.shape
    ob, oc, oh, oz = (o // tc for o in offs)
    wspec = lambda o: pl.BlockSpec((d, tc), lambda i, j, o=o: (0, o + j))
    halo = _halo_rows(conv_w.shape[0])
    return pl.pallas_call(
        functools.partial(_mix_a_kernel, *seg_info),
        grid=(m // tm, wa // tc),
        in_specs=[pl.BlockSpec((tm, d), lambda i, j: (i, 0)),
                  wspec(ob), wspec(oc), wspec(oh), wspec(oz),
                  pl.BlockSpec((conv_w.shape[0], tc), lambda i, j: (0, j))],
        out_specs=pl.BlockSpec((tm, tc), lambda i, j: (i, j)),
        out_shape=jax.ShapeDtypeStruct((m, wa), BF16),
        scratch_shapes=[pltpu.VMEM((2,) + _pad_shape(tm // 2, tc, halo), F32)],
        compiler_params=_params("arbitrary", "arbitrary"),
        name="mix_a",
    )(u, w_main, w_main, w_main, w_main, conv_w)


def _mix_c_kernel(n_prompt_tiles, seg_p, seg_s, ln_rows, u_ref, wa_ref, wg_ref, wz_ref, cw_ref,
                  cb_ref, lg_ref, lb_ref, o_ref, pad_scr, conv_scr, zs_scr):
    i = pl.program_id(0)
    j = pl.program_id(1)
    nj = pl.num_programs(1)
    seg = jnp.where(i < n_prompt_tiles, seg_p, seg_s)
    n_tc, tm, _ = conv_scr.shape
    halo = _halo_rows(cw_ref.shape[0])
    hr = tm // 2
    for hf in range(2):
        rows = slice(hf * hr, (hf + 1) * hr)
        uh = u_ref[rows, :]
        _conv_stage(pad_scr.at[hf], _dot(uh, wa_ref[...]) * jax.nn.sigmoid(_dot(uh, wg_ref[...])), seg, halo)
        zs_scr[j, rows, :] = _silu(_dot(uh, wz_ref[...])).astype(BF16)

        def emit(r0, c0, acc, hf=hf):
            conv_scr[j, hf * hr + r0:hf * hr + r0 + CONV_SUB, c0:c0 + LANES] = acc + cb_ref[:, c0:c0 + LANES]

        _conv_taps(pad_scr.at[hf], cw_ref, hr, halo, emit)

    @pl.when(j == nj - 1)
    def _():
        def body(b, carry):
            r0 = pl.multiple_of(b * ln_rows, ln_rows)
            v = jnp.concatenate([conv_scr[t, pl.ds(r0, ln_rows), :] for t in range(n_tc)], axis=1)
            zs = jnp.concatenate([zs_scr[t, pl.ds(r0, ln_rows), :] for t in range(n_tc)], axis=1)
            mu = jnp.mean(v, axis=-1, keepdims=True)
            cen = v - mu
            var = jnp.mean(cen * cen, axis=-1, keepdims=True)
            y = cen * lax.rsqrt(var + EPS) * lg_ref[...] + lb_ref[...]
            o_ref[pl.ds(r0, ln_rows), :] = (_silu(y) * zs.astype(F32)).astype(BF16)
            return carry

        lax.fori_loop(0, tm // ln_rows, body, 0)


def _mix_c_call(u, w_main, conv_w, conv_b, ln_g, ln_b, offs, wc, tm, tc, seg_info):
    m, d = u.shape
    oa, og, oz = (o // tc for o in offs)
    wspec = lambda o: pl.BlockSpec((d, tc), lambda i, j, o=o: (0, o + j))
    halo = _halo_rows(conv_w.shape[0])
    full = lambda shape: pl.BlockSpec(shape, lambda i, j: (0, 0))
    return pl.pallas_call(
        functools.partial(_mix_c_kernel, *seg_info, 16),
        grid=(m // tm, wc // tc),
        in_specs=[pl.BlockSpec((tm, d), lambda i, j: (i, 0)),
                  wspec(oa), wspec(og), wspec(oz),
                  pl.BlockSpec((conv_w.shape[0], tc), lambda i, j: (0, j)),
                  pl.BlockSpec((1, tc), lambda i, j: (0, j)),
                  full((1, wc)), full((1, wc))],
        out_specs=pl.BlockSpec((tm, wc), lambda i, j: (i, 0)),
        out_shape=jax.ShapeDtypeStruct((m, wc), BF16),
        scratch_shapes=[pltpu.VMEM((2,) + _pad_shape(tm // 2, tc, halo), F32),
                        pltpu.VMEM((wc // tc, tm, tc), F32), pltpu.VMEM((wc // tc, tm, tc), BF16)],
        compiler_params=_params("arbitrary", "arbitrary"),
        name="mix_c",
    )(u, w_main, w_main, w_main, conv_w, conv_b.reshape(1, wc), ln_g.reshape(1, wc), ln_b.reshape(1, wc))


def _xbc_kernel(n_prompt_tiles, seg_p, seg_s, u_ref, w_ref, cw_ref, cb_ref, o_ref, pad_scr):
    i = pl.program_id(0)
    seg = jnp.where(i < n_prompt_tiles, seg_p, seg_s)
    halo = _halo_rows(cw_ref.shape[0])
    hr = u_ref.shape[0] // 2
    for hf in range(2):
        _conv_stage(pad_scr.at[hf], _dot(u_ref[hf * hr:(hf + 1) * hr, :], w_ref[...]), seg, halo)

        def emit(r0, c0, acc, hf=hf):
            o_ref[hf * hr + r0:hf * hr + r0 + CONV_SUB, c0:c0 + LANES] = _silu(
                acc + cb_ref[:, c0:c0 + LANES]).astype(BF16)

        _conv_taps(pad_scr.at[hf], cw_ref, hr, halo, emit)


def _xbc_call(u, w_main, conv_w, conv_b, off, width, tm, tc, seg_info):
    m, d = u.shape
    o = off // tc
    halo = _halo_rows(conv_w.shape[0])
    return pl.pallas_call(
        functools.partial(_xbc_kernel, *seg_info),
        grid=(m // tm, width // tc),
        in_specs=[pl.BlockSpec((tm, d), lambda i, j: (i, 0)),
                  pl.BlockSpec((d, tc), lambda i, j: (0, o + j)),
                  pl.BlockSpec((conv_w.shape[0], tc), lambda i, j: (0, j)),
                  pl.BlockSpec((1, tc), lambda i, j: (0, j))],
        out_specs=pl.BlockSpec((tm, tc), lambda i, j: (i, j)),
        out_shape=jax.ShapeDtypeStruct((m, width), BF16),
        scratch_shapes=[pltpu.VMEM((2,) + _pad_shape(tm // 2, tc, halo), F32)],
        compiler_params=_params("arbitrary", "arbitrary"),
        name="xbc",
    )(u, w_main, conv_w, conv_b.reshape(1, width))


def _zdt_kernel(u_ref, wz_ref, wdt_ref, dtb_ref, zs_ref, dt_ref):
    u = u_ref[...]
    zs_ref[...] = _silu(_dot(u, wz_ref[...])).astype(BF16)

    @pl.when(pl.program_id(1) == 0)
    def _():
        v = _dot(u, wdt_ref[...]) + dtb_ref[...]
        dt_ref[...] = jnp.maximum(v, 0.0) + jnp.log1p(jnp.exp(-jnp.abs(v)))


def _zdt_call(u, w_main, w_dt, dt_bias, off, width, tm, tc):
    m, d = u.shape
    o = off // tc
    nh2 = w_dt.shape[-1]
    return pl.pallas_call(
        _zdt_kernel,
        grid=(m // tm, width // tc),
        in_specs=[pl.BlockSpec((tm, d), lambda i, j: (i, 0)),
                  pl.BlockSpec((d, tc), lambda i, j: (0, o + j)),
                  pl.BlockSpec((d, nh2), lambda i, j: (0, 0)),
                  pl.BlockSpec((1, nh2), lambda i, j: (0, 0))],
        out_specs=[pl.BlockSpec((tm, tc), lambda i, j: (i, j)),
                   pl.BlockSpec((tm, nh2), lambda i, j: (i, 0))],
        out_shape=[jax.ShapeDtypeStruct((m, width), BF16), jax.ShapeDtypeStruct((m, nh2), F32)],
        compiler_params=_params("arbitrary", "arbitrary"),
        name="zdt",
    )(u, w_main, w_dt, dt_bias.reshape(1, nh2))


def _ssd_kernel(dims, has_h0, emit_state, *refs):
    n_groups, hpg, hd, ns = dims
    refs = list(refs)
    x_ref, bm_ref, cm_ref, dt_ref, alog_ref, dsk_ref, exp_ref = refs[:7]
    refs = refs[7:]
    h0_ref = refs.pop(0) if has_h0 else None
    y_ref = refs.pop(0)
    so_ref = refs.pop(0) if emit_state else None
    state = refs.pop(0)

    d = pl.program_id(1)
    b = pl.program_id(2)
    nb = pl.num_programs(2)
    is_f = d == 0
    bt = x_ref.shape[0]
    q = SSD_CHUNK
    nck = bt // q
    gw = hpg * hd
    pairs = gw // LANES

    @pl.when(b == 0)
    def _():
        if has_h0:
            for g in range(n_groups):
                hg = h0_ref[g * hpg:(g + 1) * hpg].reshape(gw, ns)
                state[g] = hg.T
        else:
            state[...] = jnp.zeros_like(state)

    ii = lax.broadcasted_iota(jnp.int32, (q, q), 0)
    jj = lax.broadcasted_iota(jnp.int32, (q, q), 1)
    sgn = jnp.where(is_f, 1, -1)
    tri = (ii - jj) * sgn >= 0
    cum_mat = jnp.where(tri, 1.0, 0.0).astype(BF16)
    lane = lax.broadcasted_iota(jnp.int32, (q, LANES), 1)
    keep_lo = jnp.where(lane < hd, 1.0, 0.0).astype(BF16)
    keep_hi = jnp.where(lane < hd, 0.0, 1.0).astype(BF16)
    a_sel = -jnp.exp(alog_ref[pl.ds(d, 1), :])
    dsk = dsk_ref[...] * jnp.where(is_f, 1.0, 0.0)

    def chunk(k, carry):
        c = jnp.where(is_f, k, nck - 1 - k)
        r0 = pl.multiple_of(c * q, q)
        rows = pl.ds(r0, q)
        dt_c = dt_ref[rows, :]
        dt_sel = jnp.where(is_f, dt_c, pltpu.roll(dt_c, dt_c.shape[-1] // 2, 1))
        acum = _dot_exact_rhs(cum_mat, dt_sel * a_sel)
        total = jnp.where(is_f, acum[q - 1:q, :], acum[0:1, :])
        w_end = jnp.exp(total - acum) * dt_sel
        acum_t = acum.T
        dt_t = dt_sel.T
        w_end_t = w_end.T
        dec_all = jnp.exp(_dot_exact_lhs(jnp.broadcast_to(total, (8, total.shape[-1])), exp_ref[...])[0:1, :])

        for g in range(n_groups):
            cm_g = cm_ref[rows, g * ns:(g + 1) * ns]
            bm_g = bm_ref[rows, g * ns:(g + 1) * ns]
            scores = lax.dot_general(cm_g, bm_g, (((1,), (1,)), ((), ())), preferred_element_type=F32)
            cm_f = cm_g.astype(F32)
            bm_t = bm_g.astype(F32).T
            for p in range(pairs):
                lanes = slice(g * gw + p * LANES, g * gw + (p + 1) * LANES)
                x_pair = x_ref[rows, lanes]
                x_lo = x_pair * keep_lo
                x_hi = x_pair * keep_hi
                h_pair = state[g, :, p * LANES:(p + 1) * LANES]
                h_bf = h_pair.astype(BF16)
                h_lo = h_bf * keep_lo
                h_hi = h_bf * keep_hi
                m_parts, e_parts, w_parts = [], [], []
                for hh in range(2):
                    h = g * hpg + p * (LANES // hd) + hh
                    colb = jnp.broadcast_to(acum[:, h:h + 1], (q, q))
                    decay = jnp.exp(jnp.where(tri, colb - acum_t[h:h + 1, :], -jnp.inf))
                    m_parts.append((scores * decay * dt_t[h:h + 1, :]).astype(BF16))
                    e_parts.append((jnp.exp(colb) * cm_f).astype(BF16))
                    w_parts.append((bm_t * w_end_t[h:h + 1, :]).astype(BF16))
                lhs_y = jnp.concatenate(m_parts + e_parts, axis=1)
                rhs_y = jnp.concatenate([x_lo, x_hi, h_lo, h_hi], axis=0)
                y = _dot(lhs_y, rhs_y)
                s_new = _dot(jnp.concatenate(w_parts, axis=1), jnp.concatenate([x_lo, x_hi], axis=0))
                state[g, :, p * LANES:(p + 1) * LANES] = h_pair * dec_all[:, lanes] + s_new
                y_ref[rows, lanes] = y + x_pair.astype(F32) * dsk[:, lanes]
        return carry

    lax.fori_loop(0, nck, chunk, 0)

    if emit_state:
        @pl.when(b == nb - 1)
        def _():
            for g in range(n_groups):
                so_ref[g * hpg:(g + 1) * hpg] = state[g].T.reshape(hpg, hd, ns)


def _ssd_call(xbc, dt, a_log2, dsk, expand, h0, dims, row0, nseq, nblk, bt, emit_state):
    n_groups, hpg, hd, ns = dims
    heads = n_groups * hpg
    wb = heads * hd
    gn = n_groups * ns
    m = xbc.shape[0]
    blk0 = row0 // bt

    def rowblk(s, d, b):
        return blk0 + s * nblk + b + d * (nblk - 1 - 2 * b)

    in_specs = [pl.BlockSpec((bt, wb), lambda s, d, b: (rowblk(s, d, b), 0)),
                pl.BlockSpec((bt, gn), lambda s, d, b: (rowblk(s, d, b), wb // gn)),
                pl.BlockSpec((bt, gn), lambda s, d, b: (rowblk(s, d, b), wb // gn + 1)),
                pl.BlockSpec((bt, 2 * heads), lambda s, d, b: (rowblk(s, d, b), 0)),
                pl.BlockSpec((2, 2 * heads), lambda s, d, b: (0, 0)),
                pl.BlockSpec((1, wb), lambda s, d, b: (0, 0)),
                pl.BlockSpec(expand.shape, lambda s, d, b: (0, 0))]
    args = [xbc, xbc, xbc, dt, a_log2, dsk, expand]
    if h0 is not None:
        in_specs.append(pl.BlockSpec((None, None, heads, hd, ns), lambda s, d, b: (s, d, 0, 0, 0)))
        args.append(h0)
    nrows = nseq * nblk * bt
    out_specs = [pl.BlockSpec((None, bt, wb), lambda s, d, b: (d, rowblk(s, d, b) - blk0, 0))]
    out_shape = [jax.ShapeDtypeStruct((2, nrows, wb), F32)]
    if emit_state:
        out_specs.append(pl.BlockSpec((None, None, heads, hd, ns), lambda s, d, b: (s, d, 0, 0, 0)))
        out_shape.append(jax.ShapeDtypeStruct((nseq, 2, heads, hd, ns), F32))
    return pl.pallas_call(
        functools.partial(_ssd_kernel, dims, h0 is not None, emit_state),
        grid=(nseq, 2, nblk),
        in_specs=in_specs,
        out_specs=out_specs,
        out_shape=out_shape,
        scratch_shapes=[pltpu.VMEM((n_groups, ns, hpg * hd), F32)],
        compiler_params=_params("arbitrary", "arbitrary", "arbitrary"),
        name="ssd",
    )(*args)


def _ssdnorm_kernel(rows, n_prompt_tiles, yp_ref, ys_ref, zs_ref, g_ref, o_ref):
    tm = o_ref.shape[0]
    i = pl.program_id(0)

    def run(y_ref):
        def body(b, carry):
            r0 = pl.multiple_of(b * rows, rows)
            v = ((y_ref[0, pl.ds(r0, rows), :] + y_ref[1, pl.ds(r0, rows), :])
                 * zs_ref[pl.ds(r0, rows), :].astype(F32))
            o_ref[pl.ds(r0, rows), :] = (v * lax.rsqrt(jnp.mean(v * v, axis=-1, keepdims=True) + EPS)
                                         * g_ref[...]).astype(BF16)
            return carry

        lax.fori_loop(0, tm // rows, body, 0)

    pl.when(i < n_prompt_tiles)(lambda: run(yp_ref))
    pl.when(i >= n_prompt_tiles)(lambda: run(ys_ref))


def _ssdnorm_call(y_p, y_s, zs, g, tm):
    wb = zs.shape[1]
    npt = y_p.shape[1] // tm
    nst = y_s.shape[1] // tm
    return pl.pallas_call(
        functools.partial(_ssdnorm_kernel, 16, npt),
        grid=(npt + nst,),
        in_specs=[pl.BlockSpec((2, tm, wb), lambda i: (0, jnp.minimum(i, npt - 1), 0)),
                  pl.BlockSpec((2, tm, wb), lambda i: (0, jnp.maximum(i - npt, 0), 0)),
                  pl.BlockSpec((tm, wb), lambda i: (i, 0)),
                  pl.BlockSpec((1, wb), lambda i: (0, 0))],
        out_specs=pl.BlockSpec((tm, wb), lambda i: (i, 0)),
        out_shape=jax.ShapeDtypeStruct(zs.shape, BF16),
        compiler_params=_params("arbitrary"),
        name="ssdnorm",
    )(y_p, y_s, zs, g.reshape(1, wb))


def _merge_kernel(u_ref, ya_ref, yb_ref, yc_ref, wga_ref, wgb_ref, wgc_ref, wpa_ref, wpb_ref, wpc_ref, o_ref):
    u = u_ref[...]
    acc = jax.nn.sigmoid(_dot(u, wga_ref[...])) * _dot(ya_ref[...], wpa_ref[...])
    acc = acc + jax.nn.sigmoid(_dot(u, wgb_ref[...])) * _dot(yb_ref[...], wpb_ref[...])
    acc = acc + jax.nn.sigmoid(_dot(u, wgc_ref[...])) * _dot(yc_ref[...], wpc_ref[...])
    o_ref[...] = acc.astype(BF16)


def _merge_call(u, ya, ybn, yc, w_main, w_pa, w_pb, w_pc, off_g, tm, tn):
    m, d = u.shape
    og = off_g // tn
    nd = d // tn
    rowspec = lambda a: pl.BlockSpec((tm, a.shape[1]), lambda i, j: (i, 0))
    gspec = lambda k: pl.BlockSpec((d, tn), lambda i, j, k=k: (0, og + k * nd + j))
    pspec = lambda w: pl.BlockSpec((w.shape[0], tn), lambda i, j: (0, j))
    return pl.pallas_call(
        _merge_kernel,
        grid=(m // tm, nd),
        in_specs=[rowspec(u), rowspec(ya), rowspec(ybn), rowspec(yc),
                  gspec(0), gspec(1), gspec(2), pspec(w_pa), pspec(w_pb), pspec(w_pc)],
        out_specs=pl.BlockSpec((tm, tn), lambda i, j: (i, j)),
        out_shape=jax.ShapeDtypeStruct((m, d), BF16),
        compiler_params=_params("arbitrary", "arbitrary"),
        name="merge",
    )(u, ya, ybn, yc, w_main, w_main, w_main, w_pa, w_pb, w_pc)


def _final_kernel(x_ref, m_ref, wo_ref, g_ref, mod_ref, o_ref):
    d = x_ref.shape[-1]
    r = _dot(m_ref[...], wo_ref[...])
    r = r * lax.rsqrt(jnp.mean(r * r, axis=-1, keepdims=True) + EPS) * g_ref[...]
    o_ref[...] = x_ref[...] + mod_ref[:, 2 * d:3 * d] * r


def _final_call(x, mm, w_o, g_post, mod3, mod_row, tm):
    m, d = x.shape
    return pl.pallas_call(
        _final_kernel,
        grid=(m // tm,),
        in_specs=[pl.BlockSpec((tm, d), lambda i: (i, 0)),
                  pl.BlockSpec((tm, d), lambda i: (i, 0)),
                  pl.BlockSpec((d, d), lambda i: (0, 0)),
                  pl.BlockSpec((1, d), lambda i: (0, 0)),
                  pl.BlockSpec((None, 1, 3 * d), lambda i: (mod_row(i, tm), 0, 0))],
        out_specs=pl.BlockSpec((tm, d), lambda i: (i, 0)),
        out_shape=jax.ShapeDtypeStruct((m, d), F32),
        compiler_params=_params("arbitrary"),
        name="final",
    )(x, mm, w_o, g_post.reshape(1, d), mod3)


def _tile(pref, *dims):
    t = pref
    while any(dim % t for dim in dims):
        t //= 2
    return t


def kernel(x_prompt, x_sample, state_ssd, c, c_ctx, w_mod, b_mod, g_pre, g_post, w_in, conv_a_w, ssd_conv_w,
           ssd_conv_b, dt_bias, a_log, d_skip, ssd_norm_g, conf_conv_w, conf_conv_b, conf_ln_g, conf_ln_b,
           w_pa, w_pb, w_pc, w_o):
    nbp, seq, d = x_prompt.shape
    nbs, dseq, _ = x_sample.shape
    depth = w_in.shape[0]
    heads, hd, ns = state_ssd.shape[3:]
    wb = heads * hd
    xbc_w = ssd_conv_w.shape[-1]
    gn = (xbc_w - wb) // 2
    n_groups = gn // ns
    hpg = heads // n_groups
    wa = conv_a_w.shape[-1]
    wc = conf_conv_w.shape[-1]
    mp, ms = nbp * seq, nbs * dseq
    m = mp + ms
    dims = (n_groups, hpg, hd, ns)

    o_ab, o_ac, o_ah, o_az = 0, wa, 2 * wa, 3 * wa
    o_bz = 4 * wa
    o_xbc = o_bz + wb
    o_dt = o_xbc + xbc_w
    o_cglu = o_dt + 2 * heads
    w_main = jnp.concatenate([w_in[:, :, :o_dt], w_in[:, :, o_cglu:]], axis=-1).astype(BF16)
    w_dt = w_in[:, :, o_dt:o_cglu].astype(BF16)
    o_ca = o_dt
    o_cg = o_ca + wc
    o_cz = o_cg + wc
    o_g = o_cz + wc

    tm = _tile(1024, mp, ms, dseq)
    tc = _tile(512, wa, wb, wc, gn, d)
    assert (tm // 2) % seq == 0 and seq % CONV_SUB == 0 and seq & (seq - 1) == 0
    seg_info = (mp // tm, seq, GRID_W)

    def mod_row(i, t):
        return jnp.where(i < mp // t, 0, 1 + (i * t - mp) // dseq)

    rows = 8 * ((1 + nbs + 7) // 8)
    cvec = jnp.zeros((rows, d), F32).at[0].set(c_ctx).at[1:1 + nbs].set(c)
    mod_all = _mod_call(cvec, w_mod, b_mod)

    expand = (lax.broadcasted_iota(jnp.int32, (2 * heads, wb), 1) // hd
              == lax.broadcasted_iota(jnp.int32, (2 * heads, wb), 0)).astype(BF16)

    bt_p = seq
    bt_s = _tile(512, dseq)
    tmn = _tile(256, mp, ms)
    tmm = _tile(512, m)

    x = jnp.concatenate([x_prompt.reshape(mp, d), x_sample.reshape(ms, d)], axis=0)
    new_states = []
    for l in range(depth):
        mod3 = mod_all[l].reshape(rows, 1, 3 * d)
        wl = w_main[l]
        u = _premod_call(x, g_pre[l], mod3, mod_row, tm)
        ya = _mix_a_call(u, wl, conv_a_w[l], (o_ab, o_ac, o_ah, o_az), wa, tm, tc, seg_info)
        yc = _mix_c_call(u, wl, conf_conv_w[l], conf_conv_b[l], conf_ln_g[l], conf_ln_b[l],
                         (o_ca, o_cg, o_cz), wc, tm, tc, seg_info)
        xbc = _xbc_call(u, wl, ssd_conv_w[l], ssd_conv_b[l], o_xbc, xbc_w, tm, tc, seg_info)
        zs, dt = _zdt_call(u, wl, w_dt[l], dt_bias[l].reshape(-1), o_bz, wb, tm, tc)
        a_log2 = jnp.stack([a_log[l].reshape(-1), a_log[l, ::-1].reshape(-1)])
        dsk = jnp.repeat(d_skip[l], hd).reshape(1, wb)
        y_p, st = _ssd_call(xbc, dt, a_log2, dsk, expand, None, dims, 0, nbp, 1, bt_p, True)
        (y_s,) = _ssd_call(xbc, dt, a_log2, dsk, expand, state_ssd[:, l], dims, mp, nbs, dseq // bt_s, bt_s, False)
        new_states.append(st)
        ybn = _ssdnorm_call(y_p, y_s, zs, ssd_norm_g[l], tmn)
        mm = _merge_call(u, ya, ybn, yc, wl, w_pa[l].astype(BF16), w_pb[l].astype(BF16), w_pc[l].astype(BF16),
                         o_g, tmm, tc)
        x = _final_call(x, mm, w_o[l].astype(BF16), g_post[l], mod3, mod_row, tmm)
    new_state_ssd = jnp.stack(new_states, axis=1).astype(x_prompt.dtype)
    return (x[:mp].reshape(nbp, seq, d), x[mp:].reshape(nbs, dseq, d), new_state_ssd)
```
